```python
import math
import jax, jax.numpy as jnp
from jax import lax
import numpy as np

D_MODEL = 1024
BATCH = 2
SEQ = 8192
DEPTH = 1
DEC_BATCH = 128
DEC_SEQ = 4
PAST_LEN = 8192
PAGE_SIZE = 128

DK_A = 64
DV_A = 2 * DK_A
H_A = D_MODEL // (2 * DK_A)
ROPE_DIM = DK_A // 4
ROPE_THETA = 500000.0
Q_BLOCK = 128
H_R = 4
DK_R = D_MODEL // H_R
DV_R = 2 * D_MODEL // H_R
RET_THETA = 10000.0
CHUNK = 128
D_FF = 2816
EPS = 1e-6
QA_W = H_A * 2 * DK_A
KA_W = H_A * 2 * DK_A
VA_W = H_A * DV_A
QR_W = H_R * DK_R
KR_W = H_R * DK_R
VR_W = H_R * DV_R
MIX_IN_WIDTH = QA_W + KA_W + VA_W + QR_W + KR_W + 2 * VR_W + 2 * D_MODEL

kernel_name = 'hybrid_diffattn_retention_macaron_step'

F32 = jnp.float32


def rms_norm(x, g):
    xf = x.astype(F32)
    y = xf * lax.rsqrt(jnp.mean(xf * xf, axis=-1, keepdims=True) + EPS)
    return (y * g.astype(F32)).astype(x.dtype)


def rms_unit(x):
    xf = x.astype(F32)
    return (xf * lax.rsqrt(jnp.mean(xf * xf, axis=-1, keepdims=True) + EPS)).astype(x.dtype)


def rope(x, pos, rot_dim, theta):
    half = rot_dim // 2
    inv = jnp.power(theta, -jnp.arange(half, dtype=F32) * (2.0 / rot_dim))
    ang = pos.astype(F32)[:, None] * inv[None, :]
    shape = (ang.shape[0],) + (1,) * (x.ndim - 3) + (half,)
    cos = jnp.cos(ang).reshape(shape)
    sin = jnp.sin(ang).reshape(shape)
    xr = x[..., :rot_dim].astype(F32)
    x1, x2 = xr[..., :half], xr[..., half:]
    rot = jnp.concatenate([x1 * cos - x2 * sin, x2 * cos + x1 * sin], axis=-1).astype(x.dtype)
    return jnp.concatenate([rot, x[..., rot_dim:]], axis=-1)


def swiglu(x, w_in, w_out):
    a, b = jnp.split(x @ w_in, 2, axis=-1)
    return (jax.nn.silu(a) * b) @ w_out


def ffn_half(x, g_pre, g_post, w_in, w_out):
    return x + 0.5 * rms_norm(swiglu(rms_norm(x, g_pre), w_in, w_out), g_post)


def mix_project(h, pos, w_in):
    B, T, _ = h.shape
    sizes = [QA_W, KA_W, VA_W, QR_W, KR_W, VR_W, VR_W, D_MODEL, D_MODEL]
    offs = [int(o) for o in np.cumsum(sizes)[:-1]]
    q_a, k_a, v_a, q_r, k_r, v_r, g_r, gate_a, gate_r = jnp.split(h @ w_in, offs, axis=-1)
    q_a = rope(q_a.reshape(B, T, H_A, 2, DK_A), pos, ROPE_DIM, ROPE_THETA)
    k_a = rope(k_a.reshape(B, T, H_A, 2, DK_A), pos, ROPE_DIM, ROPE_THETA)
    v_a = v_a.reshape(B, T, H_A, DV_A)
    q_r = rope(q_r.reshape(B, T, H_R, DK_R), pos, DK_R, RET_THETA)
    k_r = rope(k_r.reshape(B, T, H_R, DK_R), pos, DK_R, RET_THETA) * (DK_R ** -0.5)
    v_r = v_r.reshape(B, T, H_R, DV_R)
    g_r = g_r.reshape(B, T, H_R, DV_R)
    return q_a, k_a, v_a, q_r, k_r, v_r, g_r, gate_a, gate_r


def diff_lambda(lq1, lk1, lq2, lk2, lam_init):
    return (jnp.exp(jnp.sum(lq1.astype(F32) * lk1.astype(F32)))
            - jnp.exp(jnp.sum(lq2.astype(F32) * lk2.astype(F32))) + lam_init)


def diff_attn_prompt(q, k, v, lam):
    B, T = q.shape[:2]
    nb = T // Q_BLOCK
    qb = q.reshape(B, nb, Q_BLOCK, H_A, 2, DK_A).swapaxes(0, 1)
    starts = jnp.arange(nb) * Q_BLOCK
    kpos = jnp.arange(T)
    scale = DK_A ** -0.5

    def block(args):
        qblk, i0 = args
        s = jnp.einsum('bqhcd,bkhcd->bhcqk', qblk, k).astype(F32) * scale
        qpos = i0 + jnp.arange(Q_BLOCK)
        mask = kpos[None, :] <= qpos[:, None]
        p = jax.nn.softmax(jnp.where(mask, s, -jnp.inf), axis=-1)
        a = p[:, :, 0] - lam * p[:, :, 1]
        return jnp.einsum('bhqk,bkhe->bqhe', a.astype(v.dtype), v)

    o = lax.map(block, (qb, starts))
    return o.swapaxes(0, 1).reshape(B, T, H_A, DV_A)


def diff_attn_sample(q, k, v, lam, cache_k, cache_v, layer, page_table):
    L = q.shape[1]
    past = page_table.shape[1] * cache_k.shape[2]
    kpos = jnp.arange(past + L)
    qpos = past + jnp.arange(L)
    mask = kpos[None, :] <= qpos[:, None]
    scale = DK_A ** -0.5

    def one(args):
        qs, ks, vs, pt = args
        kp = cache_k[layer, pt].reshape(past, H_A, 2, DK_A).astype(ks.dtype)
        vp = cache_v[layer, pt].reshape(past, H_A, DV_A).astype(vs.dtype)
        kk = jnp.concatenate([kp, ks], axis=0)
        vv = jnp.concatenate([vp, vs], axis=0)
        s = jnp.einsum('qhcd,khcd->hcqk', qs, kk).astype(F32) * scale
        p = jax.nn.softmax(jnp.where(mask, s, -jnp.inf), axis=-1)
        a = p[:, 0] - lam * p[:, 1]
        return jnp.einsum('hqk,khe->qhe', a.astype(vv.dtype), vv)

    return lax.map(one, (q, k, v, page_table))


def retention_log_decay():
    return jnp.log(1.0 - jnp.exp2(-5.0 - jnp.arange(H_R, dtype=F32)))


def retention_chunk(q, k, v, S, log_g):
    L = q.shape[1]
    idx = jnp.arange(L, dtype=F32)
    rel = idx[:, None] - idx[None, :]
    dmask = jnp.where(rel >= 0, jnp.exp(log_g[:, None, None] * jnp.maximum(rel, 0.0)), 0.0)
    qf, kf, vf = q.astype(F32), k.astype(F32), v.astype(F32)
    scores = jnp.einsum('blhd,bmhd->bhlm', qf, kf) * dmask[None]
    o = jnp.einsum('bhlm,bmhe->blhe', scores, vf)
    cross = jnp.exp(log_g[:, None] * (idx + 1.0)[None, :])
    o = o + jnp.einsum('blhd,bhde->blhe', qf, S) * cross.T[None, :, :, None]
    kdec = jnp.exp(log_g[:, None] * (L - 1.0 - idx)[None, :])
    S_new = (jnp.exp(log_g * L)[None, :, None, None] * S
             + jnp.einsum('blhd,blhe,hl->bhde', kf, vf, kdec))
    return o.astype(q.dtype), S_new


def retention_prompt(q, k, v, log_g):
    B, T = q.shape[:2]
    nc = T // CHUNK

    def to_chunks(a):
        return a.reshape((B, nc, CHUNK) + a.shape[2:]).swapaxes(0, 1)

    def step(S, xs):
        qc, kc, vc = xs
        o, S = retention_chunk(qc, kc, vc, S, log_g)
        return S, o

    S0 = jnp.zeros((B, H_R, DK_R, DV_R), F32)
    S, o = lax.scan(step, S0, (to_chunks(q), to_chunks(k), to_chunks(v)))
    return o.swapaxes(0, 1).reshape(B, T, H_R, DV_R), S


def layer_forward(x, pos, attend, retain, p, lam_init):
    x = ffn_half(x, p['n_pre_f1'], p['n_post_f1'], p['w_f1_in'], p['w_f1_out'])
    h = rms_norm(x, p['n_pre_m'])
    q_a, k_a, v_a, q_r, k_r, v_r, g_r, gate_a, gate_r = mix_project(h, pos, p['w_mix_in'])
    B, T, _ = x.shape
    o_a = attend(q_a, k_a, v_a)
    o_a = rms_norm(o_a, p['subln_a']) * (1.0 - lam_init)
    o_r, S = retain(q_r, k_r, v_r)
    o_r = rms_unit(o_r) * jax.nn.silu(g_r)
    y_a = o_a.reshape(B, T, VA_W) @ p['w_branch_a']
    y_r = o_r.reshape(B, T, VR_W) @ p['w_branch_r']
    m = (jax.nn.sigmoid(gate_a) * y_a + jax.nn.sigmoid(gate_r) * y_r) @ p['w_mix_out']
    x = x + rms_norm(m, p['n_post_m'])
    x = ffn_half(x, p['n_pre_f2'], p['n_post_f2'], p['w_f2_in'], p['w_f2_out'])
    return x, k_a, v_a, S.astype(x.dtype)


def setup_inputs(seed: int = 0) -> dict:
    key = jax.random.key(seed)
    ks = jax.random.split(key, 32)
    n_pages = PAST_LEN // PAGE_SIZE
    n_pool = (DEC_BATCH * n_pages * 5) // 4

    def nrm(k, shape, scale):
        return jax.random.normal(k, shape, F32) * scale

    def gain(k, n):
        return 1.0 + 0.01 * jax.random.normal(k, (DEPTH, n), F32)

    page_table = jax.random.permutation(ks[5], n_pool)[:DEC_BATCH * n_pages]
    page_table = page_table.reshape(DEC_BATCH, n_pages).astype(jnp.int32)
    return {
        'x_prompt': nrm(ks[0], (BATCH, SEQ, D_MODEL), 1.0),
        'x_sample': nrm(ks[1], (DEC_BATCH, DEC_SEQ, D_MODEL), 1.0),
        'cache_k': nrm(ks[2], (DEPTH, n_pool, PAGE_SIZE, H_A, 2, DK_A), 1.0),
        'cache_v': nrm(ks[3], (DEPTH, n_pool, PAGE_SIZE, H_A, DV_A), 1.0),
        'state_ret': nrm(ks[4], (DEPTH, DEC_BATCH, H_R, DK_R, DV_R), 0.5),
        'page_table': page_table,
        'norm_pre_ffn1': gain(ks[6], D_MODEL),
        'norm_post_ffn1': gain(ks[7], D_MODEL),
        'w_ffn1_in': nrm(ks[8], (DEPTH, D_MODEL, 2 * D_FF), D_MODEL ** -0.5),
        'w_ffn1_out': nrm(ks[9], (DEPTH, D_FF, D_MODEL), D_FF ** -0.5),
        'norm_pre_mix': gain(ks[10], D_MODEL),
        'norm_post_mix': gain(ks[11], D_MODEL),
        'w_mix_in': nrm(ks[12], (DEPTH, D_MODEL, MIX_IN_WIDTH), D_MODEL ** -0.5),
        'lambda_q1': nrm(ks[13], (DEPTH, DK_A), 0.1),
        'lambda_k1': nrm(ks[14], (DEPTH, DK_A), 0.1),
        'lambda_q2': nrm(ks[15], (DEPTH, DK_A), 0.1),
        'lambda_k2': nrm(ks[16], (DEPTH, DK_A), 0.1),
        'subln_a': gain(ks[17], DV_A),
        'w_branch_a': nrm(ks[18], (DEPTH, VA_W, D_MODEL), VA_W ** -0.5),
        'w_branch_r': nrm(ks[19], (DEPTH, VR_W, D_MODEL), VR_W ** -0.5),
        'w_mix_out': nrm(ks[20], (DEPTH, D_MODEL, D_MODEL), D_MODEL ** -0.5),
        'norm_pre_ffn2': gain(ks[21], D_MODEL),
        'norm_post_ffn2': gain(ks[22], D_MODEL),
        'w_ffn2_in': nrm(ks[23], (DEPTH, D_MODEL, 2 * D_FF), D_MODEL ** -0.5),
        'w_ffn2_out': nrm(ks[24], (DEPTH, D_FF, D_MODEL), D_FF ** -0.5),
    }


def reference(x_prompt, x_sample, cache_k, cache_v, state_ret, page_table,
              norm_pre_ffn1, norm_post_ffn1, w_ffn1_in, w_ffn1_out,
              norm_pre_mix, norm_post_mix, w_mix_in,
              lambda_q1, lambda_k1, lambda_q2, lambda_k2, subln_a,
              w_branch_a, w_branch_r, w_mix_out,
              norm_pre_ffn2, norm_post_ffn2, w_ffn2_in, w_ffn2_out):
    pos_p = jnp.arange(x_prompt.shape[1])
    past = page_table.shape[1] * cache_k.shape[2]
    pos_s = past + jnp.arange(x_sample.shape[1])
    log_g = retention_log_decay()
    yp, ys = x_prompt, x_sample
    kp_l, vp_l, sp_l, ks_l, vs_l, ss_l = [], [], [], [], [], []
    for l in range(DEPTH):
        p = {
            'n_pre_f1': norm_pre_ffn1[l], 'n_post_f1': norm_post_ffn1[l],
            'w_f1_in': w_ffn1_in[l], 'w_f1_out': w_ffn1_out[l],
            'n_pre_m': norm_pre_mix[l], 'n_post_m': norm_post_mix[l],
            'w_mix_in': w_mix_in[l], 'subln_a': subln_a[l],
            'w_branch_a': w_branch_a[l], 'w_branch_r': w_branch_r[l], 'w_mix_out': w_mix_out[l],
            'n_pre_f2': norm_pre_ffn2[l], 'n_post_f2': norm_post_ffn2[l],
            'w_f2_in': w_ffn2_in[l], 'w_f2_out': w_ffn2_out[l],
        }
        lam_init = 0.8 - 0.6 * math.exp(-0.3 * l)
        lam = diff_lambda(lambda_q1[l], lambda_k1[l], lambda_q2[l], lambda_k2[l], lam_init)
        s0 = state_ret[l].astype(F32)
        yp, kp, vp, sp = layer_forward(
            yp, pos_p,
            lambda q, k, v: diff_attn_prompt(q, k, v, lam),
            lambda q, k, v: retention_prompt(q, k, v, log_g),
            p, lam_init)
        ys, kn, vn, sn = layer_forward(
            ys, pos_s,
            lambda q, k, v: diff_attn_sample(q, k, v, lam, cache_k, cache_v, l, page_table),
            lambda q, k, v: retention_chunk(q, k, v, s0, log_g),
            p, lam_init)
        kp_l.append(kp); vp_l.append(vp); sp_l.append(sp)
        ks_l.append(kn); vs_l.append(vn); ss_l.append(sn)
    k_prompt = jnp.stack(kp_l)
    v_prompt = jnp.stack(vp_l)
    s_prompt = jnp.stack(sp_l)
    k_sample = jnp.stack(ks_l)
    v_sample = jnp.stack(vs_l)
    s_sample = jnp.stack(ss_l)
    return (yp, ys, k_prompt, v_prompt, s_prompt, k_sample, v_sample, s_sample)
```

```python
import functools
import math

import jax
import jax.numpy as jnp
from jax import lax
from jax.experimental import pallas as pl
from jax.experimental.pallas import tpu as pltpu

F32 = jnp.float32
BF16 = jnp.bfloat16

D_MODEL = 1024
D_FF = 2816
DK_A = 64
DV_A = 128
H_A = 8
ROPE_DIM = 16
ROPE_THETA = 500000.0
H_R = 4
DK_R = 256
DV_R = 512
RET_THETA = 10000.0
EPS = 1e-6
LAM_INIT = 0.8 - 0.6 * math.exp(-0.3 * 0)
LOG2E = 1.4426950408889634

LANES = 128
SUBLANES = 8
FF_CHUNK = 256
N_FF_CHUNKS = D_FF // FF_CHUNK
TM_FFN = 512
TM_MIX = 256
TQ = 512
RET_CHUNK = 256
SP = SUBLANES
PAGES_PER_STEP = 4
NEG = -1e30
VMEM_LIMIT = 56 * 1024 * 1024


def _params(sem, limit=VMEM_LIMIT):
    return pltpu.CompilerParams(dimension_semantics=sem, vmem_limit_bytes=limit)


def _resident(shape):
    nd = len(shape)
    return pl.BlockSpec(shape, lambda *_: (0,) * nd, pipeline_mode=pl.Buffered(1))


def _rms(x):
    return x * lax.rsqrt(jnp.mean(x * x, axis=-1, keepdims=True) + EPS)


def _ffn_kernel(x_ref, gpre_ref, gpost_ref, gnext_ref, win_ref, wout_ref, *rest, emit_h):
    if emit_h:
        y_ref, h_ref, acc_ref = rest
    else:
        y_ref, acc_ref = rest
    x = x_ref[...]
    xn = (_rms(x) * gpre_ref[...]).astype(BF16)
    acc_ref[...] = jnp.zeros_like(acc_ref)

    def body(c, carry):
        a = jnp.dot(xn, win_ref[0, c], preferred_element_type=F32)
        b = jnp.dot(xn, win_ref[1, c], preferred_element_type=F32)
        hid = (a * jax.nn.sigmoid(a) * b).astype(BF16)
        acc_ref[...] += jnp.dot(hid, wout_ref[c], preferred_element_type=F32)
        return carry

    lax.fori_loop(0, N_FF_CHUNKS, body, 0)
    y = x + 0.5 * (_rms(acc_ref[...]) * gpost_ref[...])
    y_ref[...] = y
    if emit_h:
        h_ref[...] = (_rms(y) * gnext_ref[...]).astype(BF16)


def _ffn(x, g_pre, g_post, g_next, w_in4, w_out3, emit_h):
    n = x.shape[0]
    tm = min(TM_FFN, n)
    row = pl.BlockSpec((tm, D_MODEL), lambda i: (i, 0))
    gain = pl.BlockSpec((1, D_MODEL), lambda i: (0, 0))
    out_shape = [jax.ShapeDtypeStruct((n, D_MODEL), F32)]
    out_specs = [row]
    if emit_h:
        out_shape.append(jax.ShapeDtypeStruct((n, D_MODEL), BF16))
        out_specs.append(row)
    return pl.pallas_call(
        functools.partial(_ffn_kernel, emit_h=emit_h),
        grid=(n // tm,),
        in_specs=[row, gain, gain, gain, _resident(w_in4.shape), _resident(w_out3.shape)],
        out_specs=out_specs,
        out_shape=out_shape,
        scratch_shapes=[pltpu.VMEM((tm, D_MODEL), F32)],
        compiler_params=_params(("parallel",)),
        name="ffn_h" if emit_h else "ffn",
    )(x, g_pre, g_post, g_next, w_in4, w_out3)


def _mixin_kernel(h_ref, w_ref, ca_ref, sa1_ref, sa2_ref, cr_ref, sr_ref,
                  qa_ref, kaf_ref, kab_ref, vaf_ref, vab_ref, qr_ref, kr_ref, vr_ref,
                  gr_ref, ga_ref, gg_ref):
    h = h_ref[...]

    def proj(off, width):
        return jnp.dot(h, w_ref[:, off:off + width], preferred_element_type=F32)

    ca, sa1, sa2 = ca_ref[...], sa1_ref[...], sa2_ref[...]
    cr, sr = cr_ref[...], sr_ref[...]

    def rope_a(z, store):
        for b in range(H_A):
            zb = z[:, b * LANES:(b + 1) * LANES]
            rot = (zb * ca + pltpu.roll(zb, LANES - ROPE_DIM // 2, 1) * sa1
                   + pltpu.roll(zb, ROPE_DIM // 2, 1) * sa2)
            store(b, rot)

    def rope_r(z, scale, out_ref):
        for hh in range(H_R):
            x1 = z[:, hh * DK_R:hh * DK_R + LANES]
            x2 = z[:, hh * DK_R + LANES:(hh + 1) * DK_R]
            out_ref[:, hh * DK_R:hh * DK_R + LANES] = ((x1 * cr - x2 * sr) * scale).astype(BF16)
            out_ref[:, hh * DK_R + LANES:(hh + 1) * DK_R] = ((x2 * cr + x1 * sr) * scale).astype(BF16)

    def store_q(b, rot):
        qa_ref[:, b * LANES:(b + 1) * LANES] = (rot * (DK_A ** -0.5 * LOG2E)).astype(BF16)

    def store_k(b, rot):
        kaf_ref[:, b * LANES:(b + 1) * LANES] = rot
        kab_ref[:, b * LANES:(b + 1) * LANES] = rot.astype(BF16)

    rope_a(proj(0, D_MODEL), store_q)
    rope_a(proj(D_MODEL, D_MODEL), store_k)
    va = proj(2 * D_MODEL, D_MODEL)
    vaf_ref[...] = va
    vab_ref[...] = va.astype(BF16)
    rope_r(proj(3 * D_MODEL, D_MODEL), 1.0, qr_ref)
    rope_r(proj(4 * D_MODEL, D_MODEL), DK_R ** -0.5, kr_ref)
    vr_ref[...] = proj(5 * D_MODEL, 2 * D_MODEL).astype(BF16)
    g = proj(7 * D_MODEL, 2 * D_MODEL)
    gr_ref[...] = g * jax.nn.sigmoid(g)
    ga_ref[...] = jax.nn.sigmoid(proj(9 * D_MODEL, D_MODEL))
    gg_ref[...] = jax.nn.sigmoid(proj(10 * D_MODEL, D_MODEL))


def _mixin(h, w_mix, tabs, period):
    n = h.shape[0]
    tm = TM_MIX
    nper = period // tm
    row = lambda w: pl.BlockSpec((tm, w), lambda i: (i, 0))
    tab = pl.BlockSpec((tm, LANES), lambda i: (i % nper, 0))
    widths = [(D_MODEL, BF16), (D_MODEL, F32), (D_MODEL, BF16), (D_MODEL, F32), (D_MODEL, BF16),
              (D_MODEL, BF16), (D_MODEL, BF16), (2 * D_MODEL, BF16), (2 * D_MODEL, F32),
              (D_MODEL, F32), (D_MODEL, F32)]
    return pl.pallas_call(
        _mixin_kernel,
        grid=(n // tm,),
        in_specs=[row(D_MODEL), _resident(w_mix.shape)] + [tab] * 5,
        out_specs=[row(w) for w, _ in widths],
        out_shape=[jax.ShapeDtypeStruct((n, w), dt) for w, dt in widths],
        compiler_params=_params(("parallel",), 60 * 1024 * 1024),
        name="mix_in",
    )(h, w_mix, *tabs)


def _rope_tables(pos):
    posf = pos.astype(F32)[:, None]
    n = pos.shape[0]
    half = ROPE_DIM // 2
    inv = jnp.power(ROPE_THETA, -jnp.arange(half, dtype=F32) * (2.0 / ROPE_DIM))
    ang = posf * inv[None, :]
    c, s = jnp.cos(ang), jnp.sin(ang)
    ones = jnp.ones((n, DK_A - ROPE_DIM), F32)
    z8 = jnp.zeros((n, half), F32)
    zrest = jnp.zeros((n, DK_A - ROPE_DIM), F32)
    ca = jnp.tile(jnp.concatenate([c, c, ones], axis=1), (1, 2))
    sa1 = jnp.tile(jnp.concatenate([-s, z8, zrest], axis=1), (1, 2))
    sa2 = jnp.tile(jnp.concatenate([z8, s, zrest], axis=1), (1, 2))
    inv_r = jnp.power(RET_THETA, -jnp.arange(DK_R // 2, dtype=F32) * (2.0 / DK_R))
    ang_r = posf * inv_r[None, :]
    return ca, sa1, sa2, jnp.cos(ang_r), jnp.sin(ang_r)


def _lambda(lq1_ref, lk1_ref, lq2_ref, lk2_ref):
    a = jnp.sum(lq1_ref[...] * lk1_ref[...], axis=-1, keepdims=True)
    b = jnp.sum(lq2_ref[...] * lk2_ref[...], axis=-1, keepdims=True)
    return jnp.exp(a) - jnp.exp(b) + LAM_INIT


def _attn_prompt_kernel(q_ref, k_ref, v_ref, lq1_ref, lk1_ref, lq2_ref, lk2_ref, g_ref,
                        o_ref, q2_scr, m_scr, l_scr, acc_scr):
    qi = pl.program_id(2)
    q = q_ref[...]
    lane = lax.broadcasted_iota(jnp.int32, q.shape, 1)
    zero = jnp.zeros_like(q)
    q2_scr[:TQ, :] = jnp.where(lane < DK_A, q, zero)
    q2_scr[TQ:, :] = jnp.where(lane >= DK_A, q, zero)
    m_scr[...] = jnp.full(m_scr.shape, NEG, F32)
    l_scr[...] = jnp.zeros_like(l_scr)
    acc_scr[...] = jnp.zeros_like(acc_scr)

    def step(j, masked):
        off = pl.multiple_of(j * TQ, TQ)
        k = k_ref[pl.ds(off, TQ), :]
        v = v_ref[pl.ds(off, TQ), :]
        s = lax.dot_general(q2_scr[...], k, (((1,), (1,)), ((), ())),
                            preferred_element_type=F32)
        if masked:
            row = lax.broadcasted_iota(jnp.int32, s.shape, 0)
            col = lax.broadcasted_iota(jnp.int32, s.shape, 1)
            s = jnp.where(col <= (row & (TQ - 1)), s, NEG)
        m_prev = m_scr[...]
        m_next = jnp.maximum(m_prev, jnp.max(s, axis=1, keepdims=True))
        p = jnp.exp2(s - jnp.tile(m_next, (1, TQ // LANES)))
        alpha = jnp.exp2(m_prev - m_next)
        l_scr[...] = alpha * l_scr[...] + jnp.sum(p, axis=1, keepdims=True)
        acc_scr[...] = alpha * acc_scr[...] + jnp.dot(p.astype(BF16), v,
                                                      preferred_element_type=F32)
        m_scr[...] = m_next

    def body(j, carry):
        step(j, False)
        return carry

    lax.fori_loop(0, qi, body, 0)
    step(qi, True)

    lam = _lambda(lq1_ref, lk1_ref, lq2_ref, lk2_ref)
    o = acc_scr[:TQ, :] / l_scr[:TQ, :] - lam * (acc_scr[TQ:, :] / l_scr[TQ:, :])
    o_ref[...] = (_rms(o) * g_ref[...] * (1.0 - LAM_INIT)).astype(BF16)


def _attn_prompt(qa, kab, vab, lams, subln, batch, seq):
    head = lambda rows, imap: pl.BlockSpec((rows, LANES), imap)
    small = lambda w: pl.BlockSpec((1, w), lambda b, h, i: (0, 0))
    nq = seq // TQ
    return pl.pallas_call(
        _attn_prompt_kernel,
        grid=(batch, H_A, nq),
        in_specs=[head(TQ, lambda b, h, i: (b * nq + i, h)),
                  head(seq, lambda b, h, i: (b, h)),
                  head(seq, lambda b, h, i: (b, h)),
                  small(DK_A), small(DK_A), small(DK_A), small(DK_A), small(DV_A)],
        out_specs=head(TQ, lambda b, h, i: (b * nq + i, h)),
        out_shape=jax.ShapeDtypeStruct((batch * seq, H_A * DV_A), BF16),
        scratch_shapes=[pltpu.VMEM((2 * TQ, LANES), BF16),
                        pltpu.VMEM((2 * TQ, LANES), F32),
                        pltpu.VMEM((2 * TQ, LANES), F32),
                        pltpu.VMEM((2 * TQ, DV_A), F32)],
        compiler_params=_params(("parallel", "parallel", "arbitrary")),
        name="attn_prompt",
    )(qa, kab, vab, *lams, subln)


def _attn_sample_kernel(pt_ref, q_ref, kn_ref, vn_ref, *rest, n_new):
    kp = rest[:PAGES_PER_STEP]
    vp = rest[PAGES_PER_STEP:2 * PAGES_PER_STEP]
    lq1_ref, lk1_ref, lq2_ref, lk2_ref, g_ref, o_ref, wq_scr, m_scr, l_scr, acc_scr = \
        rest[2 * PAGES_PER_STEP:]
    j = pl.program_id(1)
    nrow = n_new * 2 * H_A

    @pl.when(j == 0)
    def _():
        hc = lax.broadcasted_iota(jnp.int32, (2 * H_A, D_MODEL), 0)
        col = lax.broadcasted_iota(jnp.int32, (2 * H_A, D_MODEL), 1)
        own = (col // DK_A) == hc
        qf = q_ref[0].astype(F32)
        for t in range(n_new):
            qt = jnp.broadcast_to(qf[t:t + 1, :], (2 * H_A, D_MODEL))
            wq_scr[t * 2 * H_A:(t + 1) * 2 * H_A, :] = jnp.where(own, qt, 0.0)
        m_scr[...] = jnp.full(m_scr.shape, NEG, F32)
        l_scr[...] = jnp.zeros_like(l_scr)
        acc_scr[...] = jnp.zeros_like(acc_scr)

    wq = wq_scr[...]
    s = jnp.concatenate(
        [lax.dot_general(wq, kp[i][0], (((1,), (1,)), ((), ())), preferred_element_type=F32)
         for i in range(PAGES_PER_STEP)], axis=1)
    m_prev = m_scr[...]
    m_next = jnp.maximum(m_prev, jnp.max(s, axis=1, keepdims=True))
    p = jnp.exp2(s - jnp.tile(m_next, (1, s.shape[1] // LANES)))
    alpha = jnp.exp2(m_prev - m_next)
    l_scr[...] = alpha * l_scr[...] + jnp.sum(p, axis=1, keepdims=True)
    pv = jnp.dot(p[:, :LANES], vp[0][0], preferred_element_type=F32)
    for i in range(1, PAGES_PER_STEP):
        pv += jnp.dot(p[:, i * LANES:(i + 1) * LANES], vp[i][0], preferred_element_type=F32)
    acc_scr[...] = jnp.tile(alpha, (1, D_MODEL // LANES)) * acc_scr[...] + pv
    m_scr[...] = m_next

    @pl.when(j == pl.num_programs(1) - 1)
    def _():
        t_row = lax.broadcasted_iota(jnp.int32, (nrow, LANES), 0) // (2 * H_A)
        wq_f = wq_scr[...]
        s_new = []
        for jj in range(n_new):
            sj = jnp.sum(wq_f * kn_ref[0, jj:jj + 1, :], axis=1, keepdims=True)
            s_new.append(jnp.where(t_row >= jj, jnp.broadcast_to(sj, (nrow, LANES)), NEG))
        m_prev = m_scr[...]
        m_next = m_prev
        for sj in s_new:
            m_next = jnp.maximum(m_next, sj)
        alpha = jnp.exp2(m_prev - m_next)
        l = alpha * l_scr[...]
        acc = jnp.tile(alpha, (1, D_MODEL // LANES)) * acc_scr[...]
        for jj in range(n_new):
            pj = jnp.exp2(s_new[jj] - m_next)
            l = l + pj
            acc = acc + jnp.tile(pj, (1, D_MODEL // LANES)) * vn_ref[0, jj:jj + 1, :]
        lam = _lambda(lq1_ref, lk1_ref, lq2_ref, lk2_ref)
        rhc = lax.broadcasted_iota(jnp.int32, (nrow, LANES), 0) % (2 * H_A)
        coef = jnp.where(rhc % 2 == 0, 1.0, -lam) / l
        gain = g_ref[...] * (1.0 - LAM_INIT)
        out_row = lax.broadcasted_iota(jnp.int32, (SP, DV_A), 0)
        for h in range(H_A):
            blk = acc[:, h * DV_A:(h + 1) * DV_A] * jnp.where(rhc // 2 == h, coef, 0.0)
            o = jnp.zeros((SP, DV_A), F32)
            for t in range(n_new):
                ot = jnp.sum(blk[t * 2 * H_A:(t + 1) * 2 * H_A, :], axis=0, keepdims=True)
                o = jnp.where(out_row == t, jnp.broadcast_to(ot, (SP, DV_A)), o)
            o_ref[0, :, h * DV_A:(h + 1) * DV_A] = (_rms(o) * gain).astype(BF16)


def _attn_sample(qa, kaf, vaf, cache_k, cache_v, page_table, lams, subln, n_new):
    nseq, npages = page_table.shape
    nsteps = npages // PAGES_PER_STEP
    nrow = n_new * 2 * H_A
    seqblk = pl.BlockSpec((1, SP, D_MODEL), lambda b, j, pt: (b, 0, 0))
    small = lambda w: pl.BlockSpec((1, w), lambda b, j, pt: (0, 0))

    def page(i):
        return pl.BlockSpec((1,) + cache_k.shape[1:],
                            lambda b, j, pt: (pt[b, j * PAGES_PER_STEP + i], 0, 0))

    pages = [page(i) for i in range(PAGES_PER_STEP)]
    grid_spec = pltpu.PrefetchScalarGridSpec(
        num_scalar_prefetch=1,
        grid=(nseq, nsteps),
        in_specs=[seqblk, seqblk, seqblk] + pages + pages
                 + [small(DK_A)] * 4 + [small(DV_A)],
        out_specs=seqblk,
        scratch_shapes=[pltpu.VMEM((nrow, D_MODEL), F32),
                        pltpu.VMEM((nrow, LANES), F32),
                        pltpu.VMEM((nrow, LANES), F32),
                        pltpu.VMEM((nrow, D_MODEL), F32)])
    return pl.pallas_call(
        functools.partial(_attn_sample_kernel, n_new=n_new),
        grid_spec=grid_spec,
        out_shape=jax.ShapeDtypeStruct((nseq, SP, D_MODEL), BF16),
        compiler_params=_params(("parallel", "arbitrary")),
        name="attn_sample",
    )(page_table, qa.reshape(nseq, SP, D_MODEL), kaf.reshape(nseq, SP, D_MODEL),
      vaf.reshape(nseq, SP, D_MODEL), *([cache_k] * PAGES_PER_STEP),
      *([cache_v] * PAGES_PER_STEP), *lams, subln)


def _head_log_decay(h):
    hv = jnp.zeros((1, 1), F32) + h.astype(F32)
    return jnp.log(1.0 - jnp.exp2(-5.0 - hv))


def _ret_prompt_kernel(q_ref, k_ref, v_ref, g_ref, o_ref, s_ref, state, dmask, cross, kdec):
    h = pl.program_id(1)
    c = pl.program_id(2)
    L = RET_CHUNK

    @pl.when(c == 0)
    def _():
        lg = _head_log_decay(h)
        row = lax.broadcasted_iota(jnp.int32, (L, L), 0)
        col = lax.broadcasted_iota(jnp.int32, (L, L), 1)
        rel = (row - col).astype(F32)
        dmask[...] = jnp.where(rel >= 0, jnp.exp(lg * jnp.maximum(rel, 0.0)), 0.0)
        idx = lax.broadcasted_iota(jnp.int32, (L, LANES), 0).astype(F32)
        cross[...] = jnp.exp(lg * (idx + 1.0))
        kdec[...] = jnp.exp(lg * (L - 1.0 - idx))
        state[...] = jnp.zeros_like(state)

    q, k, v = q_ref[...], k_ref[...], v_ref[...]
    sc = lax.dot_general(q, k, (((1,), (1,)), ((), ())), preferred_element_type=F32) * dmask[...]
    s_old = state[...]
    o = jnp.dot(sc.astype(BF16), v, preferred_element_type=F32)
    o += jnp.dot(q, s_old.astype(BF16), preferred_element_type=F32) * jnp.tile(cross[...], (1, DV_R // LANES))
    kd = (k.astype(F32) * jnp.tile(kdec[...], (1, DK_R // LANES))).astype(BF16)
    upd = lax.dot_general(kd, v, (((0,), (0,)), ((), ())), preferred_element_type=F32)
    state[...] = cross[L - 1:L, :1] * s_old + upd
    o_ref[...] = (_rms(o) * g_ref[...]).astype(BF16)

    @pl.when(c == pl.num_programs(2) - 1)
    def _():
        s_ref[0, 0] = state[...]


def _ret_prompt(qr, kr, vr, gr, batch, seq):
    nc = seq // RET_CHUNK
    qk = pl.BlockSpec((RET_CHUNK, DK_R), lambda b, h, c: (b * nc + c, h))
    vv = pl.BlockSpec((RET_CHUNK, DV_R), lambda b, h, c: (b * nc + c, h))
    return pl.pallas_call(
        _ret_prompt_kernel,
        grid=(batch, H_R, nc),
        in_specs=[qk, qk, vv, vv],
        out_specs=[vv, pl.BlockSpec((1, 1, DK_R, DV_R), lambda b, h, c: (b, h, 0, 0))],
        out_shape=[jax.ShapeDtypeStruct((batch * seq, H_R * DV_R), BF16),
                   jax.ShapeDtypeStruct((batch, H_R, DK_R, DV_R), F32)],
        scratch_shapes=[pltpu.VMEM((DK_R, DV_R), F32),
                        pltpu.VMEM((RET_CHUNK, RET_CHUNK), F32),
                        pltpu.VMEM((RET_CHUNK, LANES), F32),
                        pltpu.VMEM((RET_CHUNK, LANES), F32)],
        compiler_params=_params(("parallel", "parallel", "arbitrary")),
        name="ret_prompt",
    )(qr, kr, vr, gr)


def _ret_sample_kernel(q_ref, k_ref, v_ref, g_ref, s0_ref, o_ref, s_ref, *, n_new):
    row = lax.broadcasted_iota(jnp.int32, (SP, SP), 0)
    col = lax.broadcasted_iota(jnp.int32, (SP, SP), 1)
    rel = (row - col).astype(F32)
    idx = lax.broadcasted_iota(jnp.int32, (SP, LANES), 0)
    idxf = idx.astype(F32)
    for h in range(H_R):
        lg = jnp.log(jnp.full((1, 1), 1.0 - 2.0 ** (-5.0 - h), F32))
        dmask = jnp.where((rel >= 0) & (col < n_new), jnp.exp(lg * jnp.maximum(rel, 0.0)), 0.0)
        cross = jnp.exp(lg * (idxf + 1.0))
        kdec = jnp.where(idx < n_new, jnp.exp(lg * (n_new - 1.0 - idxf)), 0.0)
        q = q_ref[0, :, h * DK_R:(h + 1) * DK_R]
        k = k_ref[0, :, h * DK_R:(h + 1) * DK_R]
        v = v_ref[0, :, h * DV_R:(h + 1) * DV_R]
        s0 = s0_ref[0, h]
        sc = lax.dot_general(q, k, (((1,), (1,)), ((), ())), preferred_element_type=F32) * dmask
        o = jnp.dot(sc.astype(BF16), v, preferred_element_type=F32)
        o += jnp.dot(q.astype(F32), s0, preferred_element_type=F32) * jnp.tile(cross, (1, DV_R // LANES))
        kd = (k.astype(F32) * jnp.tile(kdec, (1, DK_R // LANES))).astype(BF16)
        upd = lax.dot_general(kd, v, (((0,), (0,)), ((), ())), preferred_element_type=F32)
        s_ref[0, h] = jnp.exp(lg * float(n_new)) * s0 + upd
        o_ref[0, :, h * DV_R:(h + 1) * DV_R] = (
            _rms(o) * g_ref[0, :, h * DV_R:(h + 1) * DV_R]).astype(BF16)


def _ret_sample(qr, kr, vr, gr, state0, n_new):
    nseq = state0.shape[0]
    blk = lambda w: pl.BlockSpec((1, SP, w), lambda b: (b, 0, 0))
    st = pl.BlockSpec((1, H_R, DK_R, DV_R), lambda b: (b, 0, 0, 0))
    return pl.pallas_call(
        functools.partial(_ret_sample_kernel, n_new=n_new),
        grid=(nseq,),
        in_specs=[blk(H_R * DK_R), blk(H_R * DK_R), blk(H_R * DV_R), blk(H_R * DV_R), st],
        out_specs=[blk(H_R * DV_R), st],
        out_shape=[jax.ShapeDtypeStruct((nseq, SP, H_R * DV_R), BF16),
                   jax.ShapeDtypeStruct(state0.shape, F32)],
        compiler_params=_params(("parallel",)),
        name="ret_sample",
    )(qr.reshape(nseq, SP, -1), kr.reshape(nseq, SP, -1), vr.reshape(nseq, SP, -1),
      gr.reshape(nseq, SP, -1), state0)


def _mixout_kernel(oa_ref, or_ref, ga_ref, gg_ref, x_ref, wa_ref, wr_ref, wo_ref, gpost_ref, y_ref):
    ya = jnp.dot(oa_ref[...], wa_ref[...], preferred_element_type=F32)
    yr = jnp.dot(or_ref[...], wr_ref[...], preferred_element_type=F32)
    m = (ga_ref[...] * ya + gg_ref[...] * yr).astype(BF16)
    mo = jnp.dot(m, wo_ref[...], preferred_element_type=F32)
    y_ref[...] = x_ref[...] + _rms(mo) * gpost_ref[...]


def _mixout(oa, o_r, ga, gg, x, wa, wr, wo, g_post):
    n = x.shape[0]
    tm = min(TM_FFN, n)
    row = lambda w: pl.BlockSpec((tm, w), lambda i: (i, 0))
    return pl.pallas_call(
        _mixout_kernel,
        grid=(n // tm,),
        in_specs=[row(D_MODEL), row(2 * D_MODEL), row(D_MODEL), row(D_MODEL), row(D_MODEL),
                  _resident(wa.shape), _resident(wr.shape), _resident(wo.shape),
                  pl.BlockSpec((1, D_MODEL), lambda i: (0, 0))],
        out_specs=row(D_MODEL),
        out_shape=jax.ShapeDtypeStruct((n, D_MODEL), F32),
        compiler_params=_params(("parallel",)),
        name="mix_out",
    )(oa, o_r, ga, gg, x, wa, wr, wo, g_post)


def kernel(x_prompt, x_sample, cache_k, cache_v, state_ret, page_table, norm_pre_ffn1, norm_post_ffn1, w_ffn1_in, w_ffn1_out, norm_pre_mix, norm_post_mix, w_mix_in, lambda_q1, lambda_k1, lambda_q2, lambda_k2, subln_a, w_branch_a, w_branch_r, w_mix_out, norm_pre_ffn2, norm_post_ffn2, w_ffn2_in, w_ffn2_out):
    assert w_mix_in.shape[0] == 1, "single-layer step only"
    batch, seq, _ = x_prompt.shape
    nseq, n_new, _ = x_sample.shape
    past = page_table.shape[1] * cache_k.shape[2]

    def ffn_weights(w_in, w_out):
        w_in4 = w_in[0].astype(BF16).reshape(D_MODEL, 2, N_FF_CHUNKS, FF_CHUNK).transpose(1, 2, 0, 3)
        return w_in4, w_out[0].astype(BF16).reshape(N_FF_CHUNKS, FF_CHUNK, D_MODEL)

    w1 = ffn_weights(w_ffn1_in, w_ffn1_out)
    w2 = ffn_weights(w_ffn2_in, w_ffn2_out)
    w_mix = w_mix_in[0].astype(BF16)
    wa, wr, wo = (w[0].astype(BF16) for w in (w_branch_a, w_branch_r, w_mix_out))
    lams = (lambda_q1, lambda_k1, lambda_q2, lambda_k2)
    ck = cache_k[0].reshape(cache_k.shape[1], cache_k.shape[2], D_MODEL)
    cv = cache_v[0].reshape(cache_v.shape[1], cache_v.shape[2], D_MODEL)

    def layer(x, tabs, period, attend, retain):
        x1, h = _ffn(x, norm_pre_ffn1, norm_post_ffn1, norm_pre_mix, *w1, emit_h=True)
        qa, kaf, kab, vaf, vab, qr, kr, vr, gr, ga, gg = _mixin(h, w_mix, tabs, period)
        oa = attend(qa, kaf, kab, vaf, vab)
        o_r, s_new = retain(qr, kr, vr, gr)
        x2 = _mixout(oa, o_r, ga, gg, x1, wa, wr, wo, norm_post_mix)
        (y,) = _ffn(x2, norm_pre_ffn2, norm_post_ffn2, norm_pre_mix, *w2, emit_h=False)
        return y, kaf, vaf, s_new

    yp, kp, vp, sp = layer(
        x_prompt.reshape(batch * seq, D_MODEL), _rope_tables(jnp.arange(seq)), seq,
        lambda qa, kaf, kab, vaf, vab: _attn_prompt(qa, kab, vab, lams, subln_a, batch, seq),
        lambda qr, kr, vr, gr: _ret_prompt(qr, kr, vr, gr, batch, seq))

    xs = jnp.pad(x_sample, ((0, 0), (0, SP - n_new), (0, 0))).reshape(nseq * SP, D_MODEL)
    pos_s = past + jnp.arange(nseq * SP) % SP
    ys, kn, vn, sn = layer(
        xs, _rope_tables(pos_s), nseq * SP,
        lambda qa, kaf, kab, vaf, vab: _attn_sample(
            qa, kaf, vaf, ck, cv, page_table, lams, subln_a, n_new).reshape(nseq * SP, D_MODEL),
        lambda qr, kr, vr, gr: tuple(
            a.reshape(nseq * SP, -1) if a.ndim == 3 else a
            for a in _ret_sample(qr, kr, vr, gr, state_ret[0], n_new)))

    def unpad(a, *tail):
        return a.reshape(nseq, SP, *tail)[:, :n_new]

    return (yp.reshape(batch, seq, D_MODEL),
            unpad(ys, D_MODEL),
            kp.reshape(1, batch, seq, H_A, 2, DK_A),
            vp.reshape(1, batch, seq, H_A, DV_A),
            sp[None],
            unpad(kn, H_A, 2, DK_A)[None],
            unpad(vn, H_A, DV_A)[None],
            sn[None])
```

```python
import functools
import math

import jax
import jax.numpy as jnp
from jax import lax
from jax.experimental import pallas as pl
from jax.experimental.pallas import tpu as pltpu

F32 = jnp.float32
BF16 = jnp.bfloat16

D_MODEL = 1024
D_FF = 2816
DK_A = 64
DV_A = 128
H_A = 8
ROPE_DIM = 16
ROPE_THETA = 500000.0
H_R = 4
DK_R = 256
DV_R = 512
RET_THETA = 10000.0
EPS = 1e-6
LAM_INIT = 0.8 - 0.6 * math.exp(-0.3 * 0)
LOG2E = 1.4426950408889634

LANES = 128
SUBLANES = 8
FF_CHUNK = 256
N_FF_CHUNKS = D_FF // FF_CHUNK
TM_FFN = 512
TM_MIX = 256
TQ = 512
RET_CHUNK = 256
SP = SUBLANES
PAGES_PER_STEP = 8
SAMPLE_GROUP = 4
NEG = -1e30
VMEM_LIMIT = 56 * 1024 * 1024


def _params(sem, limit=VMEM_LIMIT):
    return pltpu.CompilerParams(dimension_semantics=sem, vmem_limit_bytes=limit)


def _resident(shape):
    nd = len(shape)
    return pl.BlockSpec(shape, lambda *_: (0,) * nd, pipeline_mode=pl.Buffered(1))


def _rms(x):
    return x * lax.rsqrt(jnp.mean(x * x, axis=-1, keepdims=True) + EPS)


def _ffn_kernel(x_ref, gpre_ref, gpost_ref, gnext_ref, win_ref, wout_ref, *rest, emit_h):
    if emit_h:
        y_ref, h_ref, acc_ref = rest
    else:
        y_ref, acc_ref = rest
    x = x_ref[...]
    xn = (_rms(x) * gpre_ref[...]).astype(BF16)
    acc_ref[...] = jnp.zeros_like(acc_ref)

    def body(c, carry):
        a = jnp.dot(xn, win_ref[0, c], preferred_element_type=F32)
        b = jnp.dot(xn, win_ref[1, c], preferred_element_type=F32)
        hid = (a * jax.nn.sigmoid(a) * b).astype(BF16)
        acc_ref[...] += jnp.dot(hid, wout_ref[c], preferred_element_type=F32)
        return carry

    lax.fori_loop(0, N_FF_CHUNKS, body, 0, unroll=True)
    y = x + 0.5 * (_rms(acc_ref[...]) * gpost_ref[...])
    y_ref[...] = y
    if emit_h:
        h_ref[...] = (_rms(y) * gnext_ref[...]).astype(BF16)


def _ffn(x, g_pre, g_post, g_next, w_in4, w_out3, emit_h):
    n = x.shape[0]
    tm = min(TM_FFN, n)
    row = pl.BlockSpec((tm, D_MODEL), lambda i: (i, 0))
    gain = pl.BlockSpec((1, D_MODEL), lambda i: (0, 0))
    out_shape = [jax.ShapeDtypeStruct((n, D_MODEL), F32)]
    out_specs = [row]
    if emit_h:
        out_shape.append(jax.ShapeDtypeStruct((n, D_MODEL), BF16))
        out_specs.append(row)
    return pl.pallas_call(
        functools.partial(_ffn_kernel, emit_h=emit_h),
        grid=(n // tm,),
        in_specs=[row, gain, gain, gain, _resident(w_in4.shape), _resident(w_out3.shape)],
        out_specs=out_specs,
        out_shape=out_shape,
        scratch_shapes=[pltpu.VMEM((tm, D_MODEL), F32)],
        compiler_params=_params(("parallel",)),
        name="ffn_h" if emit_h else "ffn",
    )(x, g_pre, g_post, g_next, w_in4, w_out3)


def _mixin_kernel(h_ref, w_ref, ca_ref, sa1_ref, sa2_ref, cr_ref, sr_ref,
                  qa_ref, kt_ref, kab_ref, vaf_ref, vab_ref, qr_ref, kr_ref, vr_ref,
                  gr_ref, ga_ref, gg_ref, kf_scr):
    h = h_ref[...]

    def proj(off, width):
        return jnp.dot(h, w_ref[:, off:off + width], preferred_element_type=F32)

    ca, sa1, sa2 = ca_ref[...], sa1_ref[...], sa2_ref[...]
    cr, sr = cr_ref[...], sr_ref[...]

    def rope_a(z, store):
        for b in range(H_A):
            zb = z[:, b * LANES:(b + 1) * LANES]
            rot = (zb * ca + pltpu.roll(zb, LANES - ROPE_DIM // 2, 1) * sa1
                   + pltpu.roll(zb, ROPE_DIM // 2, 1) * sa2)
            store(b, rot)

    def rope_r(z, scale, out_ref):
        for hh in range(H_R):
            x1 = z[:, hh * DK_R:hh * DK_R + LANES]
            x2 = z[:, hh * DK_R + LANES:(hh + 1) * DK_R]
            out_ref[:, hh * DK_R:hh * DK_R + LANES] = ((x1 * cr - x2 * sr) * scale).astype(BF16)
            out_ref[:, hh * DK_R + LANES:(hh + 1) * DK_R] = ((x2 * cr + x1 * sr) * scale).astype(BF16)

    def store_q(b, rot):
        qa_ref[:, b * LANES:(b + 1) * LANES] = (rot * (DK_A ** -0.5 * LOG2E)).astype(BF16)

    def store_k(b, rot):
        kf_scr[:, b * LANES:(b + 1) * LANES] = rot
        kab_ref[:, b * LANES:(b + 1) * LANES] = rot.astype(BF16)

    rope_a(proj(0, D_MODEL), store_q)
    rope_a(proj(D_MODEL, D_MODEL), store_k)
    kt_ref[0] = kf_scr[...].T
    va = proj(2 * D_MODEL, D_MODEL)
    vaf_ref[...] = va
    vab_ref[...] = va.astype(BF16)
    rope_r(proj(3 * D_MODEL, D_MODEL), 1.0, qr_ref)
    rope_r(proj(4 * D_MODEL, D_MODEL), DK_R ** -0.5, kr_ref)
    vr_ref[...] = proj(5 * D_MODEL, 2 * D_MODEL).astype(BF16)
    g = proj(7 * D_MODEL, 2 * D_MODEL)
    gr_ref[...] = g * jax.nn.sigmoid(g)
    ga_ref[...] = jax.nn.sigmoid(proj(9 * D_MODEL, D_MODEL))
    gg_ref[...] = jax.nn.sigmoid(proj(10 * D_MODEL, D_MODEL))


def _mixin(h, w_mix, tabs, period):
    n = h.shape[0]
    tm = min(TM_MIX, period)
    nper = period // tm
    row = lambda w: pl.BlockSpec((tm, w), lambda i: (i, 0))
    tab = pl.BlockSpec((tm, LANES), lambda i: (i % nper, 0))
    widths = [(D_MODEL, BF16), None, (D_MODEL, BF16), (D_MODEL, F32), (D_MODEL, BF16),
              (D_MODEL, BF16), (D_MODEL, BF16), (2 * D_MODEL, BF16), (2 * D_MODEL, F32),
              (D_MODEL, F32), (D_MODEL, F32)]
    out_specs = [row(w[0]) if w else
                 pl.BlockSpec((1, D_MODEL, tm), lambda i: (i // nper, 0, i % nper)) for w in widths]
    out_shape = [jax.ShapeDtypeStruct((n, w[0]), w[1]) if w else
                 jax.ShapeDtypeStruct((n // period, D_MODEL, period), F32) for w in widths]
    return pl.pallas_call(
        _mixin_kernel,
        grid=(n // tm,),
        in_specs=[row(D_MODEL), _resident(w_mix.shape)] + [tab] * 5,
        out_specs=out_specs,
        out_shape=out_shape,
        scratch_shapes=[pltpu.VMEM((tm, D_MODEL), F32)],
        compiler_params=_params(("parallel",), 60 * 1024 * 1024),
        name="mix_in",
    )(h, w_mix, *tabs)


def _rope_tables(pos):
    posf = pos.astype(F32)[:, None]
    n = pos.shape[0]
    half = ROPE_DIM // 2
    inv = jnp.power(ROPE_THETA, -jnp.arange(half, dtype=F32) * (2.0 / ROPE_DIM))
    ang = posf * inv[None, :]
    c, s = jnp.cos(ang), jnp.sin(ang)
    ones = jnp.ones((n, DK_A - ROPE_DIM), F32)
    z8 = jnp.zeros((n, half), F32)
    zrest = jnp.zeros((n, DK_A - ROPE_DIM), F32)
    ca = jnp.tile(jnp.concatenate([c, c, ones], axis=1), (1, 2))
    sa1 = jnp.tile(jnp.concatenate([-s, z8, zrest], axis=1), (1, 2))
    sa2 = jnp.tile(jnp.concatenate([z8, s, zrest], axis=1), (1, 2))
    inv_r = jnp.power(RET_THETA, -jnp.arange(DK_R // 2, dtype=F32) * (2.0 / DK_R))
    ang_r = posf * inv_r[None, :]
    return ca, sa1, sa2, jnp.cos(ang_r), jnp.sin(ang_r)


def _lambda(lq1_ref, lk1_ref, lq2_ref, lk2_ref):
    a = jnp.sum(lq1_ref[...] * lk1_ref[...], axis=-1, keepdims=True)
    b = jnp.sum(lq2_ref[...] * lk2_ref[...], axis=-1, keepdims=True)
    return jnp.exp(a) - jnp.exp(b) + LAM_INIT


def _attn_prompt_kernel(q_ref, k_ref, v_ref, lq1_ref, lk1_ref, lq2_ref, lk2_ref, g_ref,
                        o_ref, q2_scr, m_scr, l_scr, acc_scr):
    qi = pl.program_id(2)
    q = q_ref[...]
    lane = lax.broadcasted_iota(jnp.int32, q.shape, 1)
    zero = jnp.zeros_like(q)
    q2_scr[:TQ, :] = jnp.where(lane < DK_A, q, zero)
    q2_scr[TQ:, :] = jnp.where(lane >= DK_A, q, zero)
    m_scr[...] = jnp.full(m_scr.shape, NEG, F32)
    l_scr[...] = jnp.zeros_like(l_scr)
    acc_scr[...] = jnp.zeros_like(acc_scr)

    def step(off, width, masked):
        k = k_ref[pl.ds(off, width), :]
        v = v_ref[pl.ds(off, width), :]
        s = lax.dot_general(q2_scr[...], k, (((1,), (1,)), ((), ())),
                            preferred_element_type=F32)
        if masked:
            row = lax.broadcasted_iota(jnp.int32, s.shape, 0)
            col = lax.broadcasted_iota(jnp.int32, s.shape, 1)
            s = jnp.where(col <= (row & (TQ - 1)), s, NEG)
        m_prev = m_scr[...]
        m_next = jnp.maximum(m_prev, jnp.max(s, axis=1, keepdims=True))
        p = jnp.exp2(s - jnp.tile(m_next, (1, width // LANES)))
        alpha = jnp.exp2(m_prev - m_next)
        l_scr[...] = alpha * l_scr[...] + jnp.sum(p, axis=1, keepdims=True)
        acc_scr[...] = alpha * acc_scr[...] + jnp.dot(p.astype(BF16), v,
                                                      preferred_element_type=F32)
        m_scr[...] = m_next

    def pair(jj, carry):
        off = pl.multiple_of(jj * 2 * TQ, 2 * TQ)
        step(off, TQ, False)
        step(off + TQ, TQ, False)
        return carry

    lax.fori_loop(0, qi // 2, pair, 0)

    @pl.when(qi % 2 == 1)
    def _():
        step(pl.multiple_of((qi - 1) * TQ, TQ), TQ, False)

    step(pl.multiple_of(qi * TQ, TQ), TQ, True)

    lam = _lambda(lq1_ref, lk1_ref, lq2_ref, lk2_ref)
    o = acc_scr[:TQ, :] / l_scr[:TQ, :] - lam * (acc_scr[TQ:, :] / l_scr[TQ:, :])
    o_ref[...] = (_rms(o) * g_ref[...] * (1.0 - LAM_INIT)).astype(BF16)


def _attn_prompt(qa, kab, vab, lams, subln, batch, seq):
    head = lambda rows, imap: pl.BlockSpec((rows, LANES), imap)
    small = lambda w: pl.BlockSpec((1, w), lambda b, h, i: (0, 0))
    nq = seq // TQ
    return pl.pallas_call(
        _attn_prompt_kernel,
        grid=(batch, H_A, nq),
        in_specs=[head(TQ, lambda b, h, i: (b * nq + i, h)),
                  head(seq, lambda b, h, i: (b, h)),
                  head(seq, lambda b, h, i: (b, h)),
                  small(DK_A), small(DK_A), small(DK_A), small(DK_A), small(DV_A)],
        out_specs=head(TQ, lambda b, h, i: (b * nq + i, h)),
        out_shape=jax.ShapeDtypeStruct((batch * seq, H_A * DV_A), BF16),
        scratch_shapes=[pltpu.VMEM((2 * TQ, LANES), BF16),
                        pltpu.VMEM((2 * TQ, LANES), F32),
                        pltpu.VMEM((2 * TQ, LANES), F32),
                        pltpu.VMEM((2 * TQ, DV_A), F32)],
        compiler_params=_params(("parallel", "parallel", "arbitrary")),
        name="attn_prompt",
    )(qa, kab, vab, *lams, subln)


def _attn_sample_kernel(pt_ref, q_ref, kn_ref, vn_ref, *rest, n_new):
    kp = rest[:PAGES_PER_STEP]
    vp = rest[PAGES_PER_STEP:2 * PAGES_PER_STEP]
    lq1_ref, lk1_ref, lq2_ref, lk2_ref, g_ref, o_ref, wq_scr, m_scr, l_scr, acc_scr = \
        rest[2 * PAGES_PER_STEP:]
    j = pl.program_id(1)
    page = kp[0].shape[2]
    hrows = lambda h: slice(h * SP, (h + 1) * SP)
    hlanes = lambda h: slice(h * LANES, (h + 1) * LANES)
    row8 = lax.broadcasted_iota(jnp.int32, (SP, LANES), 0)

    @pl.when(j == 0)
    def _():
        qf = q_ref[0].astype(F32)
        row = lax.broadcasted_iota(jnp.int32, qf.shape, 0)
        qq = jnp.where(row < n_new, qf, pltpu.roll(qf, n_new, 0))
        lane = lax.broadcasted_iota(jnp.int32, (SP, LANES), 1)
        own = (lane // DK_A) == (row8 // n_new)
        for h in range(H_A):
            wq_scr[hrows(h), :] = jnp.where(own, qq[:, hlanes(h)], 0.0)
        m_scr[...] = jnp.full(m_scr.shape, NEG, F32)
        l_scr[...] = jnp.zeros_like(l_scr)
        acc_scr[...] = jnp.zeros_like(acc_scr)

    wq = wq_scr[...]

    def scores(pages):
        return jnp.concatenate(
            [jnp.concatenate(
                [jnp.dot(wq[hrows(h), :],
                         jnp.concatenate([kp[i][0, hlanes(h), :], kp[i + 1][0, hlanes(h), :]], axis=1),
                         preferred_element_type=F32)
                 for i in pages[::2]], axis=1)
             for h in range(H_A)], axis=0)

    groups = [range(g, g + SAMPLE_GROUP) for g in range(0, PAGES_PER_STEP, SAMPLE_GROUP)]
    m, l, acc = m_scr[...], l_scr[...], acc_scr[...]
    s_next = scores(groups[0])
    for g, pages in enumerate(groups):
        s = s_next
        if g + 1 < len(groups):
            s_next = scores(groups[g + 1])
        m_next = jnp.maximum(m, jnp.max(s, axis=1, keepdims=True))
        p = jnp.exp2(s - jnp.tile(m_next, (1, len(pages))))
        alpha = jnp.exp2(m - m_next)
        l = alpha * l + jnp.sum(p, axis=1, keepdims=True)
        m = m_next
        pv = []
        for h in range(0, H_A, 2):
            d = None
            for n, i in enumerate(pages):
                v2 = jnp.concatenate([vp[i][0, pl.ds(h, page, stride=H_A), :],
                                      vp[i][0, pl.ds(h + 1, page, stride=H_A), :]], axis=1)
                t = jnp.dot(p[h * SP:(h + 2) * SP, n * page:(n + 1) * page], v2,
                            preferred_element_type=F32)
                d = t if d is None else d + t
            pv += [d[:SP, :LANES], d[SP:, LANES:]]
        acc = alpha * acc + jnp.concatenate(pv, axis=0)
    m_scr[...], l_scr[...], acc_scr[...] = m, l, acc

    @pl.when(j == pl.num_programs(1) - 1)
    def _():
        lam = _lambda(lq1_ref, lk1_ref, lq2_ref, lk2_ref)
        gain = g_ref[...] * (1.0 - LAM_INIT)
        kn = kn_ref[0].astype(F32)
        vn = vn_ref[0]
        t_row = row8 % n_new
        for h in range(H_A):
            wq_h = wq_scr[hrows(h), :]
            s_new = []
            for jj in range(n_new):
                sj = jnp.sum(wq_h * kn[jj:jj + 1, hlanes(h)], axis=1, keepdims=True)
                s_new.append(jnp.where(t_row >= jj, jnp.broadcast_to(sj, (SP, LANES)), NEG))
            m_prev = m_scr[hrows(h), :]
            m_next = m_prev
            for sj in s_new:
                m_next = jnp.maximum(m_next, sj)
            alpha = jnp.exp2(m_prev - m_next)
            l = alpha * l_scr[hrows(h), :]
            acc = alpha * acc_scr[hrows(h), :]
            for jj in range(n_new):
                pj = jnp.exp2(s_new[jj] - m_next)
                l = l + pj
                acc = acc + pj * vn[jj:jj + 1, hlanes(h)]
            b = acc * (jnp.where(row8 < n_new, 1.0, -lam) / l)
            o = b + pltpu.roll(b, n_new, 0)
            o_ref[0, :, hlanes(h)] = (_rms(o) * gain).astype(BF16)


def _attn_sample(qa, kab, vaf, cache_kt, cache_v2, page_table, lams, subln, n_new):
    assert 2 * n_new == SP and cache_kt.shape[2] == LANES
    nseq, npages = page_table.shape
    pps = PAGES_PER_STEP
    assert npages % pps == 0
    seqblk = pl.BlockSpec((1, SP, D_MODEL), lambda b, j, pt: (b, 0, 0))
    small = lambda w: pl.BlockSpec((1, w), lambda b, j, pt: (0, 0))

    def page(arr, i):
        return pl.BlockSpec((1,) + arr.shape[1:], lambda b, j, pt: (pt[b, j * pps + i], 0, 0))

    grid_spec = pltpu.PrefetchScalarGridSpec(
        num_scalar_prefetch=1,
        grid=(nseq, npages // pps),
        in_specs=[seqblk, seqblk, seqblk]
                 + [page(cache_kt, i) for i in range(pps)]
                 + [page(cache_v2, i) for i in range(pps)]
                 + [small(DK_A)] * 4 + [small(DV_A)],
        out_specs=seqblk,
        scratch_shapes=[pltpu.VMEM((H_A * SP, LANES), F32)] * 4)
    return pl.pallas_call(
        functools.partial(_attn_sample_kernel, n_new=n_new),
        grid_spec=grid_spec,
        out_shape=jax.ShapeDtypeStruct((nseq, SP, D_MODEL), BF16),
        compiler_params=_params(("parallel", "arbitrary")),
        name="attn_sample",
    )(page_table, qa.reshape(nseq, SP, D_MODEL), kab.reshape(nseq, SP, D_MODEL),
      vaf.reshape(nseq, SP, D_MODEL), *([cache_kt] * pps), *([cache_v2] * pps), *lams, subln)


def _head_log_decay(h):
    hv = jnp.zeros((1, 1), F32) + h.astype(F32)
    return jnp.log(1.0 - jnp.exp2(-5.0 - hv))


def _ret_prompt_kernel(q_ref, k_ref, v_ref, g_ref, o_ref, s_ref, state, dmask, cross, kdec):
    h = pl.program_id(1)
    c = pl.program_id(2)
    L = RET_CHUNK

    @pl.when(c == 0)
    def _():
        lg = _head_log_decay(h)
        row = lax.broadcasted_iota(jnp.int32, (L, L), 0)
        col = lax.broadcasted_iota(jnp.int32, (L, L), 1)
        rel = (row - col).astype(F32)
        dmask[...] = jnp.where(rel >= 0, jnp.exp(lg * jnp.maximum(rel, 0.0)), 0.0)
        idx = lax.broadcasted_iota(jnp.int32, (L, LANES), 0).astype(F32)
        cross[...] = jnp.exp(lg * (idx + 1.0))
        kdec[...] = jnp.exp(lg * (L - 1.0 - idx))
        state[...] = jnp.zeros_like(state)

    q, k, v = q_ref[...], k_ref[...], v_ref[...]
    sc = lax.dot_general(q, k, (((1,), (1,)), ((), ())), preferred_element_type=F32) * dmask[...]
    s_old = state[...]
    o = jnp.dot(sc.astype(BF16), v, preferred_element_type=F32)
    o += jnp.dot(q, s_old.astype(BF16), preferred_element_type=F32) * jnp.tile(cross[...], (1, DV_R // LANES))
    kd = (k.astype(F32) * jnp.tile(kdec[...], (1, DK_R // LANES))).astype(BF16)
    upd = lax.dot_general(kd, v, (((0,), (0,)), ((), ())), preferred_element_type=F32)
    state[...] = cross[L - 1:L, :1] * s_old + upd
    o_ref[...] = (_rms(o) * g_ref[...]).astype(BF16)

    @pl.when(c == pl.num_programs(2) - 1)
    def _():
        s_ref[0, 0] = state[...]


def _ret_prompt(qr, kr, vr, gr, batch, seq):
    nc = seq // RET_CHUNK
    qk = pl.BlockSpec((RET_CHUNK, DK_R), lambda b, h, c: (b * nc + c, h))
    vv = pl.BlockSpec((RET_CHUNK, DV_R), lambda b, h, c: (b * nc + c, h))
    return pl.pallas_call(
        _ret_prompt_kernel,
        grid=(batch, H_R, nc),
        in_specs=[qk, qk, vv, vv],
        out_specs=[vv, pl.BlockSpec((1, 1, DK_R, DV_R), lambda b, h, c: (b, h, 0, 0))],
        out_shape=[jax.ShapeDtypeStruct((batch * seq, H_R * DV_R), BF16),
                   jax.ShapeDtypeStruct((batch, H_R, DK_R, DV_R), F32)],
        scratch_shapes=[pltpu.VMEM((DK_R, DV_R), F32),
                        pltpu.VMEM((RET_CHUNK, RET_CHUNK), F32),
                        pltpu.VMEM((RET_CHUNK, LANES), F32),
                        pltpu.VMEM((RET_CHUNK, LANES), F32)],
        compiler_params=_params(("parallel", "parallel", "arbitrary")),
        name="ret_prompt",
    )(qr, kr, vr, gr)


def _ret_sample_kernel(q_ref, k_ref, v_ref, g_ref, s0_ref, o_ref, s_ref, *, n_new):
    row = lax.broadcasted_iota(jnp.int32, (SP, SP), 0)
    col = lax.broadcasted_iota(jnp.int32, (SP, SP), 1)
    rel = (row - col).astype(F32)
    idx = lax.broadcasted_iota(jnp.int32, (SP, LANES), 0)
    idxf = idx.astype(F32)
    for h in range(H_R):
        lg = jnp.log(jnp.full((1, 1), 1.0 - 2.0 ** (-5.0 - h), F32))
        dmask = jnp.where((rel >= 0) & (col < n_new), jnp.exp(lg * jnp.maximum(rel, 0.0)), 0.0)
        cross = jnp.exp(lg * (idxf + 1.0))
        kdec = jnp.where(idx < n_new, jnp.exp(lg * (n_new - 1.0 - idxf)), 0.0)
        q = q_ref[0, :, h * DK_R:(h + 1) * DK_R]
        k = k_ref[0, :, h * DK_R:(h + 1) * DK_R]
        v = v_ref[0, :, h * DV_R:(h + 1) * DV_R]
        s0 = s0_ref[0, h]
        sc = lax.dot_general(q, k, (((1,), (1,)), ((), ())), preferred_element_type=F32) * dmask
        o = jnp.dot(sc.astype(BF16), v, preferred_element_type=F32)
        o += jnp.dot(q.astype(F32), s0, preferred_element_type=F32) * jnp.tile(cross, (1, DV_R // LANES))
        kd = (k.astype(F32) * jnp.tile(kdec, (1, DK_R // LANES))).astype(BF16)
        upd = lax.dot_general(kd, v, (((0,), (0,)), ((), ())), preferred_element_type=F32)
        s_ref[0, h] = jnp.exp(lg * float(n_new)) * s0 + upd
        o_ref[0, :, h * DV_R:(h + 1) * DV_R] = (
            _rms(o) * g_ref[0, :, h * DV_R:(h + 1) * DV_R]).astype(BF16)


def _ret_sample(qr, kr, vr, gr, state0, n_new):
    nseq = state0.shape[0]
    blk = lambda w: pl.BlockSpec((1, SP, w), lambda b: (b, 0, 0))
    st = pl.BlockSpec((1, H_R, DK_R, DV_R), lambda b: (b, 0, 0, 0))
    return pl.pallas_call(
        functools.partial(_ret_sample_kernel, n_new=n_new),
        grid=(nseq,),
        in_specs=[blk(H_R * DK_R), blk(H_R * DK_R), blk(H_R * DV_R), blk(H_R * DV_R), st],
        out_specs=[blk(H_R * DV_R), st],
        out_shape=[jax.ShapeDtypeStruct((nseq, SP, H_R * DV_R), BF16),
                   jax.ShapeDtypeStruct(state0.shape, F32)],
        compiler_params=_params(("parallel",)),
        name="ret_sample",
    )(qr.reshape(nseq, SP, -1), kr.reshape(nseq, SP, -1), vr.reshape(nseq, SP, -1),
      gr.reshape(nseq, SP, -1), state0)


def _mixout_kernel(oa_ref, or_ref, ga_ref, gg_ref, x_ref, wa_ref, wr_ref, wo_ref, gpost_ref, y_ref):
    ya = jnp.dot(oa_ref[...], wa_ref[...], preferred_element_type=F32)
    yr = jnp.dot(or_ref[...], wr_ref[...], preferred_element_type=F32)
    m = (ga_ref[...] * ya + gg_ref[...] * yr).astype(BF16)
    mo = jnp.dot(m, wo_ref[...], preferred_element_type=F32)
    y_ref[...] = x_ref[...] + _rms(mo) * gpost_ref[...]


def _mixout(oa, o_r, ga, gg, x, wa, wr, wo, g_post):
    n = x.shape[0]
    tm = min(TM_FFN, n)
    row = lambda w: pl.BlockSpec((tm, w), lambda i: (i, 0))
    return pl.pallas_call(
        _mixout_kernel,
        grid=(n // tm,),
        in_specs=[row(D_MODEL), row(2 * D_MODEL), row(D_MODEL), row(D_MODEL), row(D_MODEL),
                  _resident(wa.shape), _resident(wr.shape), _resident(wo.shape),
                  pl.BlockSpec((1, D_MODEL), lambda i: (0, 0))],
        out_specs=row(D_MODEL),
        out_shape=jax.ShapeDtypeStruct((n, D_MODEL), F32),
        compiler_params=_params(("parallel",)),
        name="mix_out",
    )(oa, o_r, ga, gg, x, wa, wr, wo, g_post)


def kernel(x_prompt, x_sample, cache_k, cache_v, state_ret, page_table, norm_pre_ffn1, norm_post_ffn1, w_ffn1_in, w_ffn1_out, norm_pre_mix, norm_post_mix, w_mix_in, lambda_q1, lambda_k1, lambda_q2, lambda_k2, subln_a, w_branch_a, w_branch_r, w_mix_out, norm_pre_ffn2, norm_post_ffn2, w_ffn2_in, w_ffn2_out):
    assert w_mix_in.shape[0] == 1, "single-layer step only"
    batch, seq, _ = x_prompt.shape
    nseq, n_new, _ = x_sample.shape
    past = page_table.shape[1] * cache_k.shape[2]

    def ffn_weights(w_in, w_out):
        w_in4 = w_in[0].astype(BF16).reshape(D_MODEL, 2, N_FF_CHUNKS, FF_CHUNK).transpose(1, 2, 0, 3)
        return w_in4, w_out[0].astype(BF16).reshape(N_FF_CHUNKS, FF_CHUNK, D_MODEL)

    w1 = ffn_weights(w_ffn1_in, w_ffn1_out)
    w2 = ffn_weights(w_ffn2_in, w_ffn2_out)
    w_mix = w_mix_in[0].astype(BF16)
    wa, wr, wo = (w[0].astype(BF16) for w in (w_branch_a, w_branch_r, w_mix_out))
    lams = (lambda_q1, lambda_k1, lambda_q2, lambda_k2)
    pool, page = cache_k.shape[1], cache_k.shape[2]
    ck = jnp.transpose(cache_k[0], (0, 2, 3, 4, 1)).reshape(pool, D_MODEL, page)
    cv = cache_v[0].reshape(pool, page * H_A, DV_A)

    def layer(x, tabs, period, attend, retain):
        x1, h = _ffn(x, norm_pre_ffn1, norm_post_ffn1, norm_pre_mix, *w1, emit_h=True)
        qa, kt, kab, vaf, vab, qr, kr, vr, gr, ga, gg = _mixin(h, w_mix, tabs, period)
        oa = attend(qa, kab, vaf, vab)
        o_r, s_new = retain(qr, kr, vr, gr)
        x2 = _mixout(oa, o_r, ga, gg, x1, wa, wr, wo, norm_post_mix)
        (y,) = _ffn(x2, norm_pre_ffn2, norm_post_ffn2, norm_pre_mix, *w2, emit_h=False)
        return y, kt, vaf, s_new

    yp, ktp, vp, sp = layer(
        x_prompt.reshape(batch * seq, D_MODEL), _rope_tables(jnp.arange(seq)), seq,
        lambda qa, kab, vaf, vab: _attn_prompt(qa, kab, vab, lams, subln_a, batch, seq),
        lambda qr, kr, vr, gr: _ret_prompt(qr, kr, vr, gr, batch, seq))

    xs = jnp.pad(x_sample, ((0, 0), (0, SP - n_new), (0, 0))).reshape(nseq * SP, D_MODEL)
    pos_s = past + jnp.arange(nseq * SP) % SP
    ys, kts, vn, sn = layer(
        xs, _rope_tables(pos_s), nseq * SP,
        lambda qa, kab, vaf, vab: _attn_sample(
            qa, kab, vaf, ck, cv, page_table, lams, subln_a, n_new).reshape(nseq * SP, D_MODEL),
        lambda qr, kr, vr, gr: tuple(
            a.reshape(nseq * SP, -1) if a.ndim == 3 else a
            for a in _ret_sample(qr, kr, vr, gr, state_ret[0], n_new)))

    def unpad(a, *tail):
        return a.reshape(nseq, SP, *tail)[:, :n_new]

    k_prompt = jnp.transpose(ktp.reshape(batch, H_A, 2, DK_A, seq), (0, 4, 1, 2, 3))
    k_sample = jnp.transpose(kts.reshape(H_A, 2, DK_A, nseq, SP)[..., :n_new], (3, 4, 0, 1, 2))
    return (yp.reshape(batch, seq, D_MODEL),
            unpad(ys, D_MODEL),
            k_prompt[None],
            vp.reshape(1, batch, seq, H_A, DV_A),
            sp[None],
            k_sample[None],
            unpad(vn, H_A, DV_A)[None],
            sn[None])
```

```python
import functools
import math

import jax
import jax.numpy as jnp
from jax import lax
from jax.experimental import pallas as pl
from jax.experimental.pallas import tpu as pltpu

F32 = jnp.float32
BF16 = jnp.bfloat16

D_MODEL = 1024
D_FF = 2816
DK_A = 64
DV_A = 128
H_A = 8
ROPE_DIM = 16
ROPE_THETA = 500000.0
H_R = 4
DK_R = 256
DV_R = 512
RET_THETA = 10000.0
EPS = 1e-6
LAM_INIT = 0.8 - 0.6 * math.exp(-0.3 * 0)
LOG2E = 1.4426950408889634

LANES = 128
SUBLANES = 8
FF_CHUNK = 256
N_FF_CHUNKS = D_FF // FF_CHUNK
TM_FFN = 512
TM_MIX = 256
TQ = 512
PROMPT_PARTS = 2
RET_CHUNK = 256
SP = SUBLANES
PAGES_PER_STEP = 16
SAMPLE_GROUP = 4
NEG = -1e30
VMEM_LIMIT = 56 * 1024 * 1024


def _params(sem, limit=VMEM_LIMIT):
    return pltpu.CompilerParams(dimension_semantics=sem, vmem_limit_bytes=limit)


def _resident(shape):
    nd = len(shape)
    return pl.BlockSpec(shape, lambda *_: (0,) * nd, pipeline_mode=pl.Buffered(1))


def _rms(x):
    return x * lax.rsqrt(jnp.mean(x * x, axis=-1, keepdims=True) + EPS)


def _ffn_kernel(x_ref, gpre_ref, gpost_ref, gnext_ref, win_ref, wout_ref, *rest, emit_h):
    if emit_h:
        y_ref, h_ref, acc_ref = rest
    else:
        y_ref, acc_ref = rest
    x = x_ref[...]
    xn = (_rms(x) * gpre_ref[...]).astype(BF16)
    acc_ref[...] = jnp.zeros_like(acc_ref)

    def body(c, carry):
        a = jnp.dot(xn, win_ref[0, c], preferred_element_type=F32)
        b = jnp.dot(xn, win_ref[1, c], preferred_element_type=F32)
        hid = (a * jax.nn.sigmoid(a) * b).astype(BF16)
        acc_ref[...] += jnp.dot(hid, wout_ref[c], preferred_element_type=F32)
        return carry

    lax.fori_loop(0, N_FF_CHUNKS, body, 0, unroll=True)
    y = x + 0.5 * (_rms(acc_ref[...]) * gpost_ref[...])
    y_ref[...] = y
    if emit_h:
        h_ref[...] = (_rms(y) * gnext_ref[...]).astype(BF16)


def _ffn(x, g_pre, g_post, g_next, w_in4, w_out3, emit_h):
    n = x.shape[0]
    tm = min(TM_FFN, n)
    row = pl.BlockSpec((tm, D_MODEL), lambda i: (i, 0))
    gain = pl.BlockSpec((1, D_MODEL), lambda i: (0, 0))
    out_shape = [jax.ShapeDtypeStruct((n, D_MODEL), F32)]
    out_specs = [row]
    if emit_h:
        out_shape.append(jax.ShapeDtypeStruct((n, D_MODEL), BF16))
        out_specs.append(row)
    return pl.pallas_call(
        functools.partial(_ffn_kernel, emit_h=emit_h),
        grid=(n // tm,),
        in_specs=[row, gain, gain, gain, _resident(w_in4.shape), _resident(w_out3.shape)],
        out_specs=out_specs,
        out_shape=out_shape,
        scratch_shapes=[pltpu.VMEM((tm, D_MODEL), F32)],
        compiler_params=_params(("parallel",)),
        name="ffn_h" if emit_h else "ffn",
    )(x, g_pre, g_post, g_next, w_in4, w_out3)


def _mixin_kernel(h_ref, w_ref, ca_ref, sa1_ref, sa2_ref, cr_ref, sr_ref,
                  qa_ref, kt_ref, kab_ref, vaf_ref, vab_ref, qr_ref, kr_ref, vr_ref,
                  gr_ref, ga_ref, gg_ref, kf_scr):
    h = h_ref[...]

    def proj(off, width):
        return jnp.dot(h, w_ref[:, off:off + width], preferred_element_type=F32)

    ca, sa1, sa2 = ca_ref[...], sa1_ref[...], sa2_ref[...]
    cr, sr = cr_ref[...], sr_ref[...]

    def rope_a(z, store):
        for b in range(H_A):
            zb = z[:, b * LANES:(b + 1) * LANES]
            rot = (zb * ca + pltpu.roll(zb, LANES - ROPE_DIM // 2, 1) * sa1
                   + pltpu.roll(zb, ROPE_DIM // 2, 1) * sa2)
            store(b, rot)

    def rope_r(z, scale, out_ref):
        for hh in range(H_R):
            x1 = z[:, hh * DK_R:hh * DK_R + LANES]
            x2 = z[:, hh * DK_R + LANES:(hh + 1) * DK_R]
            out_ref[:, hh * DK_R:hh * DK_R + LANES] = ((x1 * cr - x2 * sr) * scale).astype(BF16)
            out_ref[:, hh * DK_R + LANES:(hh + 1) * DK_R] = ((x2 * cr + x1 * sr) * scale).astype(BF16)

    def store_q(b, rot):
        qa_ref[:, b * LANES:(b + 1) * LANES] = (rot * (DK_A ** -0.5 * LOG2E)).astype(BF16)

    def store_k(b, rot):
        kf_scr[:, b * LANES:(b + 1) * LANES] = rot
        kab_ref[:, b * LANES:(b + 1) * LANES] = rot.astype(BF16)

    rope_a(proj(0, D_MODEL), store_q)
    rope_a(proj(D_MODEL, D_MODEL), store_k)
    kt_ref[0] = kf_scr[...].T
    va = proj(2 * D_MODEL, D_MODEL)
    vaf_ref[...] = va
    vab_ref[...] = va.astype(BF16)
    rope_r(proj(3 * D_MODEL, D_MODEL), 1.0, qr_ref)
    rope_r(proj(4 * D_MODEL, D_MODEL), DK_R ** -0.5, kr_ref)
    vr_ref[...] = proj(5 * D_MODEL, 2 * D_MODEL).astype(BF16)
    g = proj(7 * D_MODEL, 2 * D_MODEL)
    gr_ref[...] = g * jax.nn.sigmoid(g)
    ga_ref[...] = jax.nn.sigmoid(proj(9 * D_MODEL, D_MODEL))
    gg_ref[...] = jax.nn.sigmoid(proj(10 * D_MODEL, D_MODEL))


def _mixin(h, w_mix, tabs, period):
    n = h.shape[0]
    tm = min(TM_MIX, period)
    nper = period // tm
    row = lambda w: pl.BlockSpec((tm, w), lambda i: (i, 0))
    tab = pl.BlockSpec((tm, LANES), lambda i: (i % nper, 0))
    widths = [(D_MODEL, BF16), None, (D_MODEL, BF16), (D_MODEL, F32), (D_MODEL, BF16),
              (D_MODEL, BF16), (D_MODEL, BF16), (2 * D_MODEL, BF16), (2 * D_MODEL, F32),
              (D_MODEL, F32), (D_MODEL, F32)]
    out_specs = [row(w[0]) if w else
                 pl.BlockSpec((1, D_MODEL, tm), lambda i: (i // nper, 0, i % nper)) for w in widths]
    out_shape = [jax.ShapeDtypeStruct((n, w[0]), w[1]) if w else
                 jax.ShapeDtypeStruct((n // period, D_MODEL, period), F32) for w in widths]
    return pl.pallas_call(
        _mixin_kernel,
        grid=(n // tm,),
        in_specs=[row(D_MODEL), _resident(w_mix.shape)] + [tab] * 5,
        out_specs=out_specs,
        out_shape=out_shape,
        scratch_shapes=[pltpu.VMEM((tm, D_MODEL), F32)],
        compiler_params=_params(("parallel",), 60 * 1024 * 1024),
        name="mix_in",
    )(h, w_mix, *tabs)


def _rope_tables(pos):
    posf = pos.astype(F32)[:, None]
    n = pos.shape[0]
    half = ROPE_DIM // 2
    inv = jnp.power(ROPE_THETA, -jnp.arange(half, dtype=F32) * (2.0 / ROPE_DIM))
    ang = posf * inv[None, :]
    c, s = jnp.cos(ang), jnp.sin(ang)
    ones = jnp.ones((n, DK_A - ROPE_DIM), F32)
    z8 = jnp.zeros((n, half), F32)
    zrest = jnp.zeros((n, DK_A - ROPE_DIM), F32)
    ca = jnp.tile(jnp.concatenate([c, c, ones], axis=1), (1, 2))
    sa1 = jnp.tile(jnp.concatenate([-s, z8, zrest], axis=1), (1, 2))
    sa2 = jnp.tile(jnp.concatenate([z8, s, zrest], axis=1), (1, 2))
    inv_r = jnp.power(RET_THETA, -jnp.arange(DK_R // 2, dtype=F32) * (2.0 / DK_R))
    ang_r = posf * inv_r[None, :]
    return ca, sa1, sa2, jnp.cos(ang_r), jnp.sin(ang_r)


def _lambda(lq1_ref, lk1_ref, lq2_ref, lk2_ref):
    a = jnp.sum(lq1_ref[...] * lk1_ref[...], axis=-1, keepdims=True)
    b = jnp.sum(lq2_ref[...] * lk2_ref[...], axis=-1, keepdims=True)
    return jnp.exp(a) - jnp.exp(b) + LAM_INIT


def _prompt_attn(qi, part, q_ref, k_ref, v_ref, lam, gain, o_ref, q2_scr, m_scr, l_scr, acc_scr):
    @pl.when(part == 0)
    def _():
        q = q_ref[...]
        lane = lax.broadcasted_iota(jnp.int32, q.shape, 1)
        zero = jnp.zeros_like(q)
        q2_scr[:TQ, :] = jnp.where(lane < DK_A, q, zero)
        q2_scr[TQ:, :] = jnp.where(lane >= DK_A, q, zero)
        m_scr[...] = jnp.full(m_scr.shape, NEG, F32)
        l_scr[...] = jnp.zeros_like(l_scr)
        acc_scr[...] = jnp.zeros_like(acc_scr)

    def step(tile, masked):
        off = pl.multiple_of(tile * TQ, TQ)
        k = k_ref[pl.ds(off, TQ), :]
        v = v_ref[pl.ds(off, TQ), :]
        s = lax.dot_general(q2_scr[...], k, (((1,), (1,)), ((), ())),
                            preferred_element_type=F32)
        if masked:
            row = lax.broadcasted_iota(jnp.int32, s.shape, 0)
            col = lax.broadcasted_iota(jnp.int32, s.shape, 1)
            s = jnp.where(col <= (row & (TQ - 1)), s, NEG)
        m_prev = m_scr[...]
        m_next = jnp.maximum(m_prev, jnp.max(s, axis=1, keepdims=True))
        p = jnp.exp2(s - jnp.tile(m_next, (1, TQ // LANES)))
        alpha = jnp.exp2(m_prev - m_next)
        l_scr[...] = alpha * l_scr[...] + jnp.sum(p, axis=1, keepdims=True)
        acc_scr[...] = alpha * acc_scr[...] + jnp.dot(p.astype(BF16), v,
                                                      preferred_element_type=F32)
        m_scr[...] = m_next

    half = (qi + 1) // 2
    lo = part * half
    hi = half + part * (qi - half)

    def pair(jj, carry):
        step(lo + 2 * jj, False)
        step(lo + 2 * jj + 1, False)
        return carry

    lax.fori_loop(0, (hi - lo) // 2, pair, 0)

    @pl.when((hi - lo) % 2 == 1)
    def _():
        step(hi - 1, False)

    @pl.when(part == 1)
    def _():
        step(qi, True)
        o = acc_scr[:TQ, :] / l_scr[:TQ, :] - lam * (acc_scr[TQ:, :] / l_scr[TQ:, :])
        o_ref[...] = (_rms(o) * gain).astype(BF16)


def _sample_attn(sub, nsub, n_new, q_ref, kn_ref, vn_ref, kp, vp, lam, gain, o_ref,
                 wq_scr, m_scr, l_scr, acc_scr):
    page = kp[0].shape[2]
    hrows = lambda h: slice(h * SP, (h + 1) * SP)
    hlanes = lambda h: slice(h * LANES, (h + 1) * LANES)
    row8 = lax.broadcasted_iota(jnp.int32, (SP, LANES), 0)

    @pl.when(sub == 0)
    def _():
        qf = q_ref[0].astype(F32)
        row = lax.broadcasted_iota(jnp.int32, qf.shape, 0)
        qq = jnp.where(row < n_new, qf, pltpu.roll(qf, n_new, 0))
        lane = lax.broadcasted_iota(jnp.int32, (SP, LANES), 1)
        own = (lane // DK_A) == (row8 // n_new)
        for h in range(H_A):
            wq_scr[hrows(h), :] = jnp.where(own, qq[:, hlanes(h)], 0.0)
        m_scr[...] = jnp.full(m_scr.shape, NEG, F32)
        l_scr[...] = jnp.zeros_like(l_scr)
        acc_scr[...] = jnp.zeros_like(acc_scr)

    wq = wq_scr[...]

    def scores(pages):
        return jnp.concatenate(
            [jnp.concatenate(
                [jnp.dot(wq[hrows(h), :],
                         jnp.concatenate([kp[i][0, hlanes(h), :], kp[i + 1][0, hlanes(h), :]], axis=1),
                         preferred_element_type=F32)
                 for i in pages[::2]], axis=1)
             for h in range(H_A)], axis=0)

    groups = [range(g, g + SAMPLE_GROUP) for g in range(0, len(kp), SAMPLE_GROUP)]
    m, l, acc = m_scr[...], l_scr[...], acc_scr[...]
    s_next = scores(groups[0])
    for g, pages in enumerate(groups):
        s = s_next
        if g + 1 < len(groups):
            s_next = scores(groups[g + 1])
        m_next = jnp.maximum(m, jnp.max(s, axis=1, keepdims=True))
        p = jnp.exp2(s - jnp.tile(m_next, (1, len(pages))))
        alpha = jnp.exp2(m - m_next)
        l = alpha * l + jnp.sum(p, axis=1, keepdims=True)
        m = m_next
        pv = []
        for h in range(0, H_A, 2):
            d = None
            for n, i in enumerate(pages):
                v2 = jnp.concatenate([vp[i][0, pl.ds(h, page, stride=H_A), :],
                                      vp[i][0, pl.ds(h + 1, page, stride=H_A), :]], axis=1)
                t = jnp.dot(p[h * SP:(h + 2) * SP, n * page:(n + 1) * page], v2,
                            preferred_element_type=F32)
                d = t if d is None else d + t
            pv += [d[:SP, :LANES], d[SP:, LANES:]]
        acc = alpha * acc + jnp.concatenate(pv, axis=0)
    m_scr[...], l_scr[...], acc_scr[...] = m, l, acc

    @pl.when(sub == nsub - 1)
    def _():
        kn = kn_ref[0].astype(F32)
        vn = vn_ref[0]
        t_row = row8 % n_new
        for h in range(H_A):
            wq_h = wq_scr[hrows(h), :]
            s_new = []
            for jj in range(n_new):
                sj = jnp.sum(wq_h * kn[jj:jj + 1, hlanes(h)], axis=1, keepdims=True)
                s_new.append(jnp.where(t_row >= jj, jnp.broadcast_to(sj, (SP, LANES)), NEG))
            m_prev = m_scr[hrows(h), :]
            m_next = m_prev
            for sj in s_new:
                m_next = jnp.maximum(m_next, sj)
            alpha = jnp.exp2(m_prev - m_next)
            l = alpha * l_scr[hrows(h), :]
            acc = alpha * acc_scr[hrows(h), :]
            for jj in range(n_new):
                pj = jnp.exp2(s_new[jj] - m_next)
                l = l + pj
                acc = acc + pj * vn[jj:jj + 1, hlanes(h)]
            b = acc * (jnp.where(row8 < n_new, 1.0, -lam) / l)
            o = b + pltpu.roll(b, n_new, 0)
            o_ref[0, :, hlanes(h)] = (_rms(o) * gain).astype(BF16)


def _attn_kernel(pt_ref, q_ref, k_ref, v_ref, qs_ref, kn_ref, vn_ref, *rest, n_new, nsub):
    pps = PAGES_PER_STEP
    kp, vp = rest[:pps], rest[pps:2 * pps]
    (lq1_ref, lk1_ref, lq2_ref, lk2_ref, g_ref, o_ref, os_ref,
     q2_scr, m_scr, l_scr, acc_scr, wq_scr, ms_scr, ls_scr, accs_scr) = rest[2 * pps:]
    qi, part = pl.program_id(2), pl.program_id(3)
    step = ((pl.program_id(0) * pl.num_programs(1) + pl.program_id(1)) * pl.num_programs(2)
            + qi) * PROMPT_PARTS + part
    lam = _lambda(lq1_ref, lk1_ref, lq2_ref, lk2_ref)
    gain = g_ref[...] * (1.0 - LAM_INIT)
    _prompt_attn(qi, part, q_ref, k_ref, v_ref, lam, gain, o_ref, q2_scr, m_scr, l_scr, acc_scr)
    _sample_attn(step % nsub, nsub, n_new, qs_ref, kn_ref, vn_ref, kp, vp, lam, gain, os_ref,
                 wq_scr, ms_scr, ls_scr, accs_scr)


def _attn(qa, kab, vab, qa_s, kab_s, vaf_s, cache_kt, cache_v2, page_table, lams, subln,
          batch, seq, n_new):
    assert 2 * n_new == SP and cache_kt.shape[2] == LANES
    nseq, npages = page_table.shape
    pps = PAGES_PER_STEP
    nq = seq // TQ
    nsteps = batch * H_A * nq * PROMPT_PARTS
    nsub = npages // pps
    assert npages % pps == 0 and nseq * nsub == nsteps, "decode pages must tile the prompt grid"
    flat = lambda b, h, i, r: ((b * H_A + h) * nq + i) * PROMPT_PARTS + r
    head = lambda rows, imap: pl.BlockSpec((rows, LANES), imap)
    seqblk = pl.BlockSpec((1, SP, D_MODEL),
                          lambda b, h, i, r, pt: (flat(b, h, i, r) // nsub, 0, 0))
    small = lambda w: pl.BlockSpec((1, w), lambda b, h, i, r, pt: (0, 0))

    def page(arr, n):
        def imap(b, h, i, r, pt):
            t = flat(b, h, i, r)
            return (pt[t // nsub, (t % nsub) * pps + n], 0, 0)
        return pl.BlockSpec((1,) + arr.shape[1:], imap)

    grid_spec = pltpu.PrefetchScalarGridSpec(
        num_scalar_prefetch=1,
        grid=(batch, H_A, nq, PROMPT_PARTS),
        in_specs=[head(TQ, lambda b, h, i, r, pt: (b * nq + i, h)),
                  head(seq, lambda b, h, i, r, pt: (b, h)),
                  head(seq, lambda b, h, i, r, pt: (b, h)),
                  seqblk, seqblk, seqblk]
                 + [page(cache_kt, n) for n in range(pps)]
                 + [page(cache_v2, n) for n in range(pps)]
                 + [small(DK_A)] * 4 + [small(DV_A)],
        out_specs=[head(TQ, lambda b, h, i, r, pt: (b * nq + i, h)), seqblk],
        scratch_shapes=[pltpu.VMEM((2 * TQ, LANES), BF16),
                        pltpu.VMEM((2 * TQ, LANES), F32),
                        pltpu.VMEM((2 * TQ, LANES), F32),
                        pltpu.VMEM((2 * TQ, DV_A), F32)]
                       + [pltpu.VMEM((H_A * SP, LANES), F32)] * 4)
    return pl.pallas_call(
        functools.partial(_attn_kernel, n_new=n_new, nsub=nsub),
        grid_spec=grid_spec,
        out_shape=[jax.ShapeDtypeStruct((batch * seq, H_A * DV_A), BF16),
                   jax.ShapeDtypeStruct((nseq, SP, D_MODEL), BF16)],
        compiler_params=_params(("arbitrary",) * 4),
        name="attn",
    )(page_table, qa, kab, vab, qa_s.reshape(nseq, SP, D_MODEL), kab_s.reshape(nseq, SP, D_MODEL),
      vaf_s.reshape(nseq, SP, D_MODEL), *([cache_kt] * pps), *([cache_v2] * pps), *lams, subln)


def _head_log_decay(h):
    hv = jnp.zeros((1, 1), F32) + h.astype(F32)
    return jnp.log(1.0 - jnp.exp2(-5.0 - hv))


def _ret_prompt_kernel(q_ref, k_ref, v_ref, g_ref, o_ref, s_ref, state, dmask, cross, kdec):
    h = pl.program_id(1)
    c = pl.program_id(2)
    L = RET_CHUNK

    @pl.when(c == 0)
    def _():
        lg = _head_log_decay(h)
        row = lax.broadcasted_iota(jnp.int32, (L, L), 0)
        col = lax.broadcasted_iota(jnp.int32, (L, L), 1)
        rel = (row - col).astype(F32)
        dmask[...] = jnp.where(rel >= 0, jnp.exp(lg * jnp.maximum(rel, 0.0)), 0.0)
        idx = lax.broadcasted_iota(jnp.int32, (L, LANES), 0).astype(F32)
        cross[...] = jnp.exp(lg * (idx + 1.0))
        kdec[...] = jnp.exp(lg * (L - 1.0 - idx))
        state[...] = jnp.zeros_like(state)

    q, k, v = q_ref[...], k_ref[...], v_ref[...]
    sc = lax.dot_general(q, k, (((1,), (1,)), ((), ())), preferred_element_type=F32) * dmask[...]
    s_old = state[...]
    o = jnp.dot(sc.astype(BF16), v, preferred_element_type=F32)
    o += jnp.dot(q, s_old.astype(BF16), preferred_element_type=F32) * jnp.tile(cross[...], (1, DV_R // LANES))
    kd = (k.astype(F32) * jnp.tile(kdec[...], (1, DK_R // LANES))).astype(BF16)
    upd = lax.dot_general(kd, v, (((0,), (0,)), ((), ())), preferred_element_type=F32)
    state[...] = cross[L - 1:L, :1] * s_old + upd
    o_ref[...] = (_rms(o) * g_ref[...]).astype(BF16)

    @pl.when(c == pl.num_programs(2) - 1)
    def _():
        s_ref[0, 0] = state[...]


def _ret_prompt(qr, kr, vr, gr, batch, seq):
    nc = seq // RET_CHUNK
    qk = pl.BlockSpec((RET_CHUNK, DK_R), lambda b, h, c: (b * nc + c, h))
    vv = pl.BlockSpec((RET_CHUNK, DV_R), lambda b, h, c: (b * nc + c, h))
    return pl.pallas_call(
        _ret_prompt_kernel,
        grid=(batch, H_R, nc),
        in_specs=[qk, qk, vv, vv],
        out_specs=[vv, pl.BlockSpec((1, 1, DK_R, DV_R), lambda b, h, c: (b, h, 0, 0))],
        out_shape=[jax.ShapeDtypeStruct((batch * seq, H_R * DV_R), BF16),
                   jax.ShapeDtypeStruct((batch, H_R, DK_R, DV_R), F32)],
        scratch_shapes=[pltpu.VMEM((DK_R, DV_R), F32),
                        pltpu.VMEM((RET_CHUNK, RET_CHUNK), F32),
                        pltpu.VMEM((RET_CHUNK, LANES), F32),
                        pltpu.VMEM((RET_CHUNK, LANES), F32)],
        compiler_params=_params(("parallel", "parallel", "arbitrary")),
        name="ret_prompt",
    )(qr, kr, vr, gr)


def _ret_sample_kernel(q_ref, k_ref, v_ref, g_ref, s0_ref, o_ref, s_ref, *, n_new):
    row = lax.broadcasted_iota(jnp.int32, (SP, SP), 0)
    col = lax.broadcasted_iota(jnp.int32, (SP, SP), 1)
    rel = (row - col).astype(F32)
    idx = lax.broadcasted_iota(jnp.int32, (SP, LANES), 0)
    idxf = idx.astype(F32)
    for h in range(H_R):
        lg = jnp.log(jnp.full((1, 1), 1.0 - 2.0 ** (-5.0 - h), F32))
        dmask = jnp.where((rel >= 0) & (col < n_new), jnp.exp(lg * jnp.maximum(rel, 0.0)), 0.0)
        cross = jnp.exp(lg * (idxf + 1.0))
        kdec = jnp.where(idx < n_new, jnp.exp(lg * (n_new - 1.0 - idxf)), 0.0)
        q = q_ref[0, :, h * DK_R:(h + 1) * DK_R]
        k = k_ref[0, :, h * DK_R:(h + 1) * DK_R]
        v = v_ref[0, :, h * DV_R:(h + 1) * DV_R]
        s0 = s0_ref[0, h]
        sc = lax.dot_general(q, k, (((1,), (1,)), ((), ())), preferred_element_type=F32) * dmask
        o = jnp.dot(sc.astype(BF16), v, preferred_element_type=F32)
        o += jnp.dot(q.astype(F32), s0, preferred_element_type=F32) * jnp.tile(cross, (1, DV_R // LANES))
        kd = (k.astype(F32) * jnp.tile(kdec, (1, DK_R // LANES))).astype(BF16)
        upd = lax.dot_general(kd, v, (((0,), (0,)), ((), ())), preferred_element_type=F32)
        s_ref[0, h] = jnp.exp(lg * float(n_new)) * s0 + upd
        o_ref[0, :, h * DV_R:(h + 1) * DV_R] = (
            _rms(o) * g_ref[0, :, h * DV_R:(h + 1) * DV_R]).astype(BF16)


def _ret_sample(qr, kr, vr, gr, state0, n_new):
    nseq = state0.shape[0]
    blk = lambda w: pl.BlockSpec((1, SP, w), lambda b: (b, 0, 0))
    st = pl.BlockSpec((1, H_R, DK_R, DV_R), lambda b: (b, 0, 0, 0))
    return pl.pallas_call(
        functools.partial(_ret_sample_kernel, n_new=n_new),
        grid=(nseq,),
        in_specs=[blk(H_R * DK_R), blk(H_R * DK_R), blk(H_R * DV_R), blk(H_R * DV_R), st],
        out_specs=[blk(H_R * DV_R), st],
        out_shape=[jax.ShapeDtypeStruct((nseq, SP, H_R * DV_R), BF16),
                   jax.ShapeDtypeStruct(state0.shape, F32)],
        compiler_params=_params(("parallel",)),
        name="ret_sample",
    )(qr.reshape(nseq, SP, -1), kr.reshape(nseq, SP, -1), vr.reshape(nseq, SP, -1),
      gr.reshape(nseq, SP, -1), state0)


def _mixout_kernel(oa_ref, or_ref, ga_ref, gg_ref, x_ref, wa_ref, wr_ref, wo_ref, gpost_ref, y_ref):
    ya = jnp.dot(oa_ref[...], wa_ref[...], preferred_element_type=F32)
    yr = jnp.dot(or_ref[...], wr_ref[...], preferred_element_type=F32)
    m = (ga_ref[...] * ya + gg_ref[...] * yr).astype(BF16)
    mo = jnp.dot(m, wo_ref[...], preferred_element_type=F32)
    y_ref[...] = x_ref[...] + _rms(mo) * gpost_ref[...]


def _mixout(oa, o_r, ga, gg, x, wa, wr, wo, g_post):
    n = x.shape[0]
    tm = min(TM_FFN, n)
    row = lambda w: pl.BlockSpec((tm, w), lambda i: (i, 0))
    return pl.pallas_call(
        _mixout_kernel,
        grid=(n // tm,),
        in_specs=[row(D_MODEL), row(2 * D_MODEL), row(D_MODEL), row(D_MODEL), row(D_MODEL),
                  _resident(wa.shape), _resident(wr.shape), _resident(wo.shape),
                  pl.BlockSpec((1, D_MODEL), lambda i: (0, 0))],
        out_specs=row(D_MODEL),
        out_shape=jax.ShapeDtypeStruct((n, D_MODEL), F32),
        compiler_params=_params(("parallel",)),
        name="mix_out",
    )(oa, o_r, ga, gg, x, wa, wr, wo, g_post)


def kernel(x_prompt, x_sample, cache_k, cache_v, state_ret, page_table, norm_pre_ffn1, norm_post_ffn1, w_ffn1_in, w_ffn1_out, norm_pre_mix, norm_post_mix, w_mix_in, lambda_q1, lambda_k1, lambda_q2, lambda_k2, subln_a, w_branch_a, w_branch_r, w_mix_out, norm_pre_ffn2, norm_post_ffn2, w_ffn2_in, w_ffn2_out):
    assert w_mix_in.shape[0] == 1, "single-layer step only"
    batch, seq, _ = x_prompt.shape
    nseq, n_new, _ = x_sample.shape
    past = page_table.shape[1] * cache_k.shape[2]

    def ffn_weights(w_in, w_out):
        w_in4 = w_in[0].astype(BF16).reshape(D_MODEL, 2, N_FF_CHUNKS, FF_CHUNK).transpose(1, 2, 0, 3)
        return w_in4, w_out[0].astype(BF16).reshape(N_FF_CHUNKS, FF_CHUNK, D_MODEL)

    w1 = ffn_weights(w_ffn1_in, w_ffn1_out)
    w2 = ffn_weights(w_ffn2_in, w_ffn2_out)
    w_mix = w_mix_in[0].astype(BF16)
    wa, wr, wo = (w[0].astype(BF16) for w in (w_branch_a, w_branch_r, w_mix_out))
    lams = (lambda_q1, lambda_k1, lambda_q2, lambda_k2)
    pool, page = cache_k.shape[1], cache_k.shape[2]
    ck = jnp.transpose(cache_k[0], (0, 2, 3, 4, 1)).reshape(pool, D_MODEL, page)
    cv = cache_v[0].reshape(pool, page * H_A, DV_A)

    def pre(x, tabs, period):
        x1, h = _ffn(x, norm_pre_ffn1, norm_post_ffn1, norm_pre_mix, *w1, emit_h=True)
        return x1, _mixin(h, w_mix, tabs, period)

    def post(x1, oa, o_r, ga, gg):
        x2 = _mixout(oa, o_r, ga, gg, x1, wa, wr, wo, norm_post_mix)
        return _ffn(x2, norm_pre_ffn2, norm_post_ffn2, norm_pre_mix, *w2, emit_h=False)[0]

    xs = jnp.pad(x_sample, ((0, 0), (0, SP - n_new), (0, 0))).reshape(nseq * SP, D_MODEL)
    pos_s = past + jnp.arange(nseq * SP) % SP
    x1s, (qa_s, kts, kab_s, vn, _, qr_s, kr_s, vr_s, gr_s, ga_s, gg_s) = pre(
        xs, _rope_tables(pos_s), nseq * SP)
    x1p, (qa_p, ktp, kab_p, vp, vab_p, qr_p, kr_p, vr_p, gr_p, ga_p, gg_p) = pre(
        x_prompt.reshape(batch * seq, D_MODEL), _rope_tables(jnp.arange(seq)), seq)

    oa_p, oa_s = _attn(qa_p, kab_p, vab_p, qa_s, kab_s, vn, ck, cv, page_table, lams, subln_a,
                       batch, seq, n_new)
    or_p, sp = _ret_prompt(qr_p, kr_p, vr_p, gr_p, batch, seq)
    or_s, sn = _ret_sample(qr_s, kr_s, vr_s, gr_s, state_ret[0], n_new)

    yp = post(x1p, oa_p, or_p, ga_p, gg_p)
    ys = post(x1s, oa_s.reshape(nseq * SP, D_MODEL), or_s.reshape(nseq * SP, -1), ga_s, gg_s)

    def unpad(a, *tail):
        return a.reshape(nseq, SP, *tail)[:, :n_new]

    k_prompt = jnp.transpose(ktp.reshape(batch, H_A, 2, DK_A, seq), (0, 4, 1, 2, 3))
    k_sample = jnp.transpose(kts.reshape(H_A, 2, DK_A, nseq, SP)[..., :n_new], (3, 4, 0, 1, 2))
    return (yp.reshape(batch, seq, D_MODEL),
            unpad(ys, D_MODEL),
            k_prompt[None],
            vp.reshape(1, batch, seq, H_A, DV_A),
            sp[None],
            k_sample[None],
            unpad(vn, H_A, DV_A)[None],
            sn[None])
```

```python
import functools
import math

import jax
import jax.numpy as jnp
from jax import lax
from jax.experimental import pallas as pl
from jax.experimental.pallas import tpu as pltpu

F32 = jnp.float32
BF16 = jnp.bfloat16

D_MODEL = 1024
D_FF = 2816
DK_A = 64
DV_A = 128
H_A = 8
ROPE_DIM = 16
ROPE_THETA = 500000.0
H_R = 4
DK_R = 256
DV_R = 512
RET_THETA = 10000.0
EPS = 1e-6
LAM_INIT = 0.8 - 0.6 * math.exp(-0.3 * 0)
LOG2E = 1.4426950408889634

LANES = 128
SUBLANES = 8
FF_CHUNK = 256
N_FF_CHUNKS = D_FF // FF_CHUNK
TM_FFN = 512
TM_MIX = 256
TQ = 512
PROMPT_PARTS = 2
RET_CHUNK = 256
RET_SAMPLE_HEADS = 2
SP = SUBLANES
PAGES_PER_STEP = 16
SAMPLE_GROUP = 4
NEG = -1e30
VMEM_LIMIT = 56 * 1024 * 1024


def _params(sem, limit=VMEM_LIMIT):
    return pltpu.CompilerParams(dimension_semantics=sem, vmem_limit_bytes=limit)


def _resident(shape):
    nd = len(shape)
    return pl.BlockSpec(shape, lambda *_: (0,) * nd, pipeline_mode=pl.Buffered(1))


def _rms(x):
    return x * lax.rsqrt(jnp.mean(x * x, axis=-1, keepdims=True) + EPS)


def _ffn_kernel(x_ref, gpre_ref, gpost_ref, gnext_ref, win_ref, wout_ref, *rest, emit_h):
    if emit_h:
        y_ref, h_ref, acc_ref = rest
    else:
        y_ref, acc_ref = rest
    x = x_ref[...]
    xn = (_rms(x) * gpre_ref[...]).astype(BF16)
    acc_ref[...] = jnp.zeros_like(acc_ref)

    for c in range(N_FF_CHUNKS):
        lo = c * FF_CHUNK
        a = jnp.dot(xn, win_ref[:, lo:lo + FF_CHUNK], preferred_element_type=F32)
        b = jnp.dot(xn, win_ref[:, D_FF + lo:D_FF + lo + FF_CHUNK], preferred_element_type=F32)
        hid = (a * jax.nn.sigmoid(a) * b).astype(BF16)
        acc_ref[...] += jnp.dot(hid, wout_ref[lo:lo + FF_CHUNK, :], preferred_element_type=F32)
    y = x + 0.5 * (_rms(acc_ref[...]) * gpost_ref[...])
    y_ref[...] = y
    if emit_h:
        h_ref[...] = (_rms(y) * gnext_ref[...]).astype(BF16)


def _ffn(x, g_pre, g_post, g_next, w_in, w_out, emit_h):
    n = x.shape[0]
    tm = min(TM_FFN, n)
    row = pl.BlockSpec((tm, D_MODEL), lambda i: (i, 0))
    gain = pl.BlockSpec((1, D_MODEL), lambda i: (0, 0))
    out_shape = [jax.ShapeDtypeStruct((n, D_MODEL), F32)]
    out_specs = [row]
    if emit_h:
        out_shape.append(jax.ShapeDtypeStruct((n, D_MODEL), BF16))
        out_specs.append(row)
    return pl.pallas_call(
        functools.partial(_ffn_kernel, emit_h=emit_h),
        grid=(n // tm,),
        in_specs=[row, gain, gain, gain, _resident(w_in.shape), _resident(w_out.shape)],
        out_specs=out_specs,
        out_shape=out_shape,
        scratch_shapes=[pltpu.VMEM((tm, D_MODEL), F32)],
        compiler_params=_params(("parallel",)),
        name="ffn_h" if emit_h else "ffn",
    )(x, g_pre, g_post, g_next, w_in, w_out)


def _mixin_kernel(h_ref, w_ref, ca_ref, sa1_ref, sa2_ref, cr_ref, sr_ref,
                  qa_ref, kt_ref, kab_ref, vaf_ref, vab_ref, qr_ref, kr_ref, vr_ref,
                  gr_ref, ga_ref, gg_ref, kf_scr):
    h = h_ref[...]

    def proj(off, width):
        return jnp.dot(h, w_ref[:, off:off + width], preferred_element_type=F32)

    ca, sa1, sa2 = ca_ref[...], sa1_ref[...], sa2_ref[...]
    cr, sr = cr_ref[...], sr_ref[...]

    def rope_a(z, store):
        for b in range(H_A):
            zb = z[:, b * LANES:(b + 1) * LANES]
            rot = (zb * ca + pltpu.roll(zb, LANES - ROPE_DIM // 2, 1) * sa1
                   + pltpu.roll(zb, ROPE_DIM // 2, 1) * sa2)
            store(b, rot)

    def rope_r(z, scale, out_ref):
        for hh in range(H_R):
            x1 = z[:, hh * DK_R:hh * DK_R + LANES]
            x2 = z[:, hh * DK_R + LANES:(hh + 1) * DK_R]
            out_ref[:, hh * DK_R:hh * DK_R + LANES] = ((x1 * cr - x2 * sr) * scale).astype(BF16)
            out_ref[:, hh * DK_R + LANES:(hh + 1) * DK_R] = ((x2 * cr + x1 * sr) * scale).astype(BF16)

    def store_q(b, rot):
        qa_ref[:, b * LANES:(b + 1) * LANES] = (rot * (DK_A ** -0.5 * LOG2E)).astype(BF16)

    def store_k(b, rot):
        kf_scr[:, b * LANES:(b + 1) * LANES] = rot
        kab_ref[:, b * LANES:(b + 1) * LANES] = rot.astype(BF16)

    rope_a(proj(0, D_MODEL), store_q)
    rope_a(proj(D_MODEL, D_MODEL), store_k)
    kt_ref[0] = kf_scr[...].T
    va = proj(2 * D_MODEL, D_MODEL)
    vaf_ref[...] = va
    vab_ref[...] = va.astype(BF16)
    rope_r(proj(3 * D_MODEL, D_MODEL), 1.0, qr_ref)
    rope_r(proj(4 * D_MODEL, D_MODEL), DK_R ** -0.5, kr_ref)
    vr_ref[...] = proj(5 * D_MODEL, 2 * D_MODEL).astype(BF16)
    g = proj(7 * D_MODEL, 2 * D_MODEL)
    gr_ref[...] = g * jax.nn.sigmoid(g)
    ga_ref[...] = jax.nn.sigmoid(proj(9 * D_MODEL, D_MODEL))
    gg_ref[...] = jax.nn.sigmoid(proj(10 * D_MODEL, D_MODEL))


def _mixin(h, w_mix, tabs, period):
    n = h.shape[0]
    tm = min(TM_MIX, period)
    nper = period // tm
    row = lambda w: pl.BlockSpec((tm, w), lambda i: (i, 0))
    tab = pl.BlockSpec((tm, LANES), lambda i: (i % nper, 0))
    widths = [(D_MODEL, BF16), None, (D_MODEL, BF16), (D_MODEL, F32), (D_MODEL, BF16),
              (D_MODEL, BF16), (D_MODEL, BF16), (2 * D_MODEL, BF16), (2 * D_MODEL, F32),
              (D_MODEL, F32), (D_MODEL, F32)]
    out_specs = [row(w[0]) if w else
                 pl.BlockSpec((1, D_MODEL, tm), lambda i: (i // nper, 0, i % nper)) for w in widths]
    out_shape = [jax.ShapeDtypeStruct((n, w[0]), w[1]) if w else
                 jax.ShapeDtypeStruct((n // period, D_MODEL, period), F32) for w in widths]
    return pl.pallas_call(
        _mixin_kernel,
        grid=(n // tm,),
        in_specs=[row(D_MODEL), _resident(w_mix.shape)] + [tab] * 5,
        out_specs=out_specs,
        out_shape=out_shape,
        scratch_shapes=[pltpu.VMEM((tm, D_MODEL), F32)],
        compiler_params=_params(("parallel",), 60 * 1024 * 1024),
        name="mix_in",
    )(h, w_mix, *tabs)


def _rope_tables(pos):
    posf = pos.astype(F32)[:, None]
    n = pos.shape[0]
    half = ROPE_DIM // 2
    inv = jnp.power(ROPE_THETA, -jnp.arange(half, dtype=F32) * (2.0 / ROPE_DIM))
    ang = posf * inv[None, :]
    c, s = jnp.cos(ang), jnp.sin(ang)
    ones = jnp.ones((n, DK_A - ROPE_DIM), F32)
    z8 = jnp.zeros((n, half), F32)
    zrest = jnp.zeros((n, DK_A - ROPE_DIM), F32)
    ca = jnp.tile(jnp.concatenate([c, c, ones], axis=1), (1, 2))
    sa1 = jnp.tile(jnp.concatenate([-s, z8, zrest], axis=1), (1, 2))
    sa2 = jnp.tile(jnp.concatenate([z8, s, zrest], axis=1), (1, 2))
    inv_r = jnp.power(RET_THETA, -jnp.arange(DK_R // 2, dtype=F32) * (2.0 / DK_R))
    ang_r = posf * inv_r[None, :]
    return ca, sa1, sa2, jnp.cos(ang_r), jnp.sin(ang_r)


def _lambda(lq1_ref, lk1_ref, lq2_ref, lk2_ref):
    a = jnp.sum(lq1_ref[...] * lk1_ref[...], axis=-1, keepdims=True)
    b = jnp.sum(lq2_ref[...] * lk2_ref[...], axis=-1, keepdims=True)
    return jnp.exp(a) - jnp.exp(b) + LAM_INIT


def _prompt_attn(qi, part, q_ref, k_ref, v_ref, lam, gain, o_ref, q2_scr, m_scr, l_scr, acc_scr):
    @pl.when(part == 0)
    def _():
        q = q_ref[...]
        lane = lax.broadcasted_iota(jnp.int32, q.shape, 1)
        zero = jnp.zeros_like(q)
        q2_scr[:TQ, :] = jnp.where(lane < DK_A, q, zero)
        q2_scr[TQ:, :] = jnp.where(lane >= DK_A, q, zero)
        m_scr[...] = jnp.full(m_scr.shape, NEG, F32)
        l_scr[...] = jnp.zeros_like(l_scr)
        acc_scr[...] = jnp.zeros_like(acc_scr)

    def step(tile, masked):
        off = pl.multiple_of(tile * TQ, TQ)
        k = k_ref[pl.ds(off, TQ), :]
        v = v_ref[pl.ds(off, TQ), :]
        s = lax.dot_general(q2_scr[...], k, (((1,), (1,)), ((), ())),
                            preferred_element_type=F32)
        if masked:
            row = lax.broadcasted_iota(jnp.int32, s.shape, 0)
            col = lax.broadcasted_iota(jnp.int32, s.shape, 1)
            s = jnp.where(col <= (row & (TQ - 1)), s, NEG)
        m_prev = m_scr[...]
        m_next = jnp.maximum(m_prev, jnp.max(s, axis=1, keepdims=True))
        p = jnp.exp2(s - jnp.tile(m_next, (1, TQ // LANES)))
        alpha = jnp.exp2(m_prev - m_next)
        l_scr[...] = alpha * l_scr[...] + jnp.sum(p, axis=1, keepdims=True)
        acc_scr[...] = alpha * acc_scr[...] + jnp.dot(p.astype(BF16), v,
                                                      preferred_element_type=F32)
        m_scr[...] = m_next

    half = (qi + 1) // 2
    lo = part * half
    hi = half + part * (qi - half)

    def pair(jj, carry):
        step(lo + 2 * jj, False)
        step(lo + 2 * jj + 1, False)
        return carry

    lax.fori_loop(0, (hi - lo) // 2, pair, 0)

    @pl.when((hi - lo) % 2 == 1)
    def _():
        step(hi - 1, False)

    @pl.when(part == 1)
    def _():
        step(qi, True)
        o = acc_scr[:TQ, :] / l_scr[:TQ, :] - lam * (acc_scr[TQ:, :] / l_scr[TQ:, :])
        o_ref[...] = (_rms(o) * gain).astype(BF16)


def _sample_attn(sub, nsub, n_new, q_ref, kn_ref, vn_ref, kp, vp, lam, gain, o_ref,
                 wq_scr, m_scr, l_scr, acc_scr):
    page = kp[0].shape[2]
    hrows = lambda h: slice(h * SP, (h + 1) * SP)
    hlanes = lambda h: slice(h * LANES, (h + 1) * LANES)
    row8 = lax.broadcasted_iota(jnp.int32, (SP, LANES), 0)

    @pl.when(sub == 0)
    def _():
        qf = q_ref[0].astype(F32)
        row = lax.broadcasted_iota(jnp.int32, qf.shape, 0)
        qq = jnp.where(row < n_new, qf, pltpu.roll(qf, n_new, 0))
        lane = lax.broadcasted_iota(jnp.int32, (SP, LANES), 1)
        own = (lane // DK_A) == (row8 // n_new)
        for h in range(H_A):
            wq_scr[hrows(h), :] = jnp.where(own, qq[:, hlanes(h)], 0.0)
        m_scr[...] = jnp.full(m_scr.shape, NEG, F32)
        l_scr[...] = jnp.zeros_like(l_scr)
        acc_scr[...] = jnp.zeros_like(acc_scr)

    wq = wq_scr[...]

    def scores(pages):
        return jnp.concatenate(
            [jnp.concatenate(
                [jnp.dot(wq[hrows(h), :],
                         jnp.concatenate([kp[i][0, hlanes(h), :], kp[i + 1][0, hlanes(h), :]], axis=1),
                         preferred_element_type=F32)
                 for i in pages[::2]], axis=1)
             for h in range(H_A)], axis=0)

    groups = [range(g, g + SAMPLE_GROUP) for g in range(0, len(kp), SAMPLE_GROUP)]
    m, l, acc = m_scr[...], l_scr[...], acc_scr[...]
    s_next = scores(groups[0])
    for g, pages in enumerate(groups):
        s = s_next
        if g + 1 < len(groups):
            s_next = scores(groups[g + 1])
        m_next = jnp.maximum(m, jnp.max(s, axis=1, keepdims=True))
        p = jnp.exp2(s - jnp.tile(m_next, (1, len(pages))))
        alpha = jnp.exp2(m - m_next)
        l = alpha * l + jnp.sum(p, axis=1, keepdims=True)
        m = m_next
        pv = []
        for h in range(0, H_A, 2):
            d = None
            for n, i in enumerate(pages):
                v2 = jnp.concatenate([vp[i][0, pl.ds(h, page, stride=H_A), :],
                                      vp[i][0, pl.ds(h + 1, page, stride=H_A), :]], axis=1)
                t = jnp.dot(p[h * SP:(h + 2) * SP, n * page:(n + 1) * page], v2,
                            preferred_element_type=F32)
                d = t if d is None else d + t
            pv += [d[:SP, :LANES], d[SP:, LANES:]]
        acc = alpha * acc + jnp.concatenate(pv, axis=0)
    m_scr[...], l_scr[...], acc_scr[...] = m, l, acc

    @pl.when(sub == nsub - 1)
    def _():
        kn = kn_ref[0].astype(F32)
        vn = vn_ref[0]
        t_row = row8 % n_new
        for h in range(H_A):
            wq_h = wq_scr[hrows(h), :]
            s_new = []
            for jj in range(n_new):
                sj = jnp.sum(wq_h * kn[jj:jj + 1, hlanes(h)], axis=1, keepdims=True)
                s_new.append(jnp.where(t_row >= jj, jnp.broadcast_to(sj, (SP, LANES)), NEG))
            m_prev = m_scr[hrows(h), :]
            m_next = m_prev
            for sj in s_new:
                m_next = jnp.maximum(m_next, sj)
            alpha = jnp.exp2(m_prev - m_next)
            l = alpha * l_scr[hrows(h), :]
            acc = alpha * acc_scr[hrows(h), :]
            for jj in range(n_new):
                pj = jnp.exp2(s_new[jj] - m_next)
                l = l + pj
                acc = acc + pj * vn[jj:jj + 1, hlanes(h)]
            b = acc * (jnp.where(row8 < n_new, 1.0, -lam) / l)
            o = b + pltpu.roll(b, n_new, 0)
            o_ref[0, :, hlanes(h)] = (_rms(o) * gain).astype(BF16)


def _attn_kernel(pt_ref, q_ref, k_ref, v_ref, qs_ref, kn_ref, vn_ref, *rest, n_new, nsub, nq):
    pps = PAGES_PER_STEP
    kp, vp = rest[:pps], rest[pps:2 * pps]
    (lq1_ref, lk1_ref, lq2_ref, lk2_ref, g_ref, o_ref, os_ref,
     q2_scr, m_scr, l_scr, acc_scr, wq_scr, ms_scr, ls_scr, accs_scr) = rest[2 * pps:]
    step = pl.program_id(0)
    qi, part = (step // PROMPT_PARTS) % nq, step % PROMPT_PARTS
    lam = _lambda(lq1_ref, lk1_ref, lq2_ref, lk2_ref)
    gain = g_ref[...] * (1.0 - LAM_INIT)
    _prompt_attn(qi, part, q_ref, k_ref, v_ref, lam, gain, o_ref, q2_scr, m_scr, l_scr, acc_scr)
    _sample_attn(step % nsub, nsub, n_new, qs_ref, kn_ref, vn_ref, kp, vp, lam, gain, os_ref,
                 wq_scr, ms_scr, ls_scr, accs_scr)


def _attn(qa, kab, vab, qa_s, kab_s, vaf_s, cache_kt, cache_v2, page_table, lams, subln,
          batch, seq, n_new):
    assert 2 * n_new == SP and cache_kt.shape[2] == LANES
    nseq, npages = page_table.shape
    pps = PAGES_PER_STEP
    nq = seq // TQ
    nsteps = batch * H_A * nq * PROMPT_PARTS
    nsub = npages // pps
    assert npages % pps == 0 and nseq * nsub == nsteps, "decode pages must tile the prompt grid"
    tile = lambda t: ((t // (PROMPT_PARTS * nq * H_A)) * nq + (t // PROMPT_PARTS) % nq,
                      (t // (PROMPT_PARTS * nq)) % H_A)
    bh = lambda t: (t // (PROMPT_PARTS * nq * H_A), (t // (PROMPT_PARTS * nq)) % H_A)
    head = lambda rows, imap: pl.BlockSpec((rows, LANES), imap)
    seqblk = pl.BlockSpec((1, SP, D_MODEL), lambda t, pt: (t // nsub, 0, 0))
    small = lambda w: pl.BlockSpec((1, w), lambda t, pt: (0, 0))

    def page(arr, n):
        return pl.BlockSpec((1,) + arr.shape[1:], lambda t, pt: (pt[t * pps + n], 0, 0))

    grid_spec = pltpu.PrefetchScalarGridSpec(
        num_scalar_prefetch=1,
        grid=(nsteps,),
        in_specs=[head(TQ, lambda t, pt: tile(t)),
                  head(seq, lambda t, pt: bh(t)),
                  head(seq, lambda t, pt: bh(t)),
                  seqblk, seqblk, seqblk]
                 + [page(cache_kt, n) for n in range(pps)]
                 + [page(cache_v2, n) for n in range(pps)]
                 + [small(DK_A)] * 4 + [small(DV_A)],
        out_specs=[head(TQ, lambda t, pt: tile(t)), seqblk],
        scratch_shapes=[pltpu.VMEM((2 * TQ, LANES), BF16),
                        pltpu.VMEM((2 * TQ, LANES), F32),
                        pltpu.VMEM((2 * TQ, LANES), F32),
                        pltpu.VMEM((2 * TQ, DV_A), F32)]
                       + [pltpu.VMEM((H_A * SP, LANES), F32)] * 4)
    return pl.pallas_call(
        functools.partial(_attn_kernel, n_new=n_new, nsub=nsub, nq=nq),
        grid_spec=grid_spec,
        out_shape=[jax.ShapeDtypeStruct((batch * seq, H_A * DV_A), BF16),
                   jax.ShapeDtypeStruct((nseq, SP, D_MODEL), BF16)],
        compiler_params=_params(("arbitrary",)),
        name="attn",
    )(page_table.reshape(-1), qa, kab, vab, qa_s.reshape(nseq, SP, D_MODEL), kab_s.reshape(nseq, SP, D_MODEL),
      vaf_s.reshape(nseq, SP, D_MODEL), *([cache_kt] * pps), *([cache_v2] * pps), *lams, subln)


def _head_log_decay(h):
    hv = jnp.zeros((1, 1), F32) + h.astype(F32)
    return jnp.log(1.0 - jnp.exp2(-5.0 - hv))


def _ret_prompt_chunk(h, c, last, q_ref, k_ref, v_ref, g_ref, o_ref, s_ref, state, dmask, cross, kdec):
    L = RET_CHUNK

    @pl.when(c == 0)
    def _():
        lg = _head_log_decay(h)
        row = lax.broadcasted_iota(jnp.int32, (L, L), 0)
        col = lax.broadcasted_iota(jnp.int32, (L, L), 1)
        rel = (row - col).astype(F32)
        dmask[...] = jnp.where(rel >= 0, jnp.exp(lg * jnp.maximum(rel, 0.0)), 0.0)
        idx = lax.broadcasted_iota(jnp.int32, (L, LANES), 0).astype(F32)
        cross[...] = jnp.exp(lg * (idx + 1.0))
        kdec[...] = jnp.exp(lg * (L - 1.0 - idx))
        state[...] = jnp.zeros_like(state)

    q, k, v = q_ref[...], k_ref[...], v_ref[...]
    sc = lax.dot_general(q, k, (((1,), (1,)), ((), ())), preferred_element_type=F32) * dmask[...]
    s_old = state[...]
    o = jnp.dot(sc.astype(BF16), v, preferred_element_type=F32)
    o += jnp.dot(q, s_old.astype(BF16), preferred_element_type=F32) * jnp.tile(cross[...], (1, DV_R // LANES))
    kd = (k.astype(F32) * jnp.tile(kdec[...], (1, DK_R // LANES))).astype(BF16)
    upd = lax.dot_general(kd, v, (((0,), (0,)), ((), ())), preferred_element_type=F32)
    state[...] = cross[L - 1:L, :1] * s_old + upd
    o_ref[...] = (_rms(o) * g_ref[...]).astype(BF16)

    @pl.when(c == last)
    def _():
        s_ref[0, 0] = state[...]


def _ret_sample_heads(head0, n_new, q_ref, k_ref, v_ref, g_ref, s0_ref, o_ref, s_ref):
    row = lax.broadcasted_iota(jnp.int32, (SP, SP), 0)
    col = lax.broadcasted_iota(jnp.int32, (SP, SP), 1)
    rel = (row - col).astype(F32)
    idx = lax.broadcasted_iota(jnp.int32, (SP, LANES), 0)
    idxf = idx.astype(F32)
    for h in range(s0_ref.shape[1]):
        lg = _head_log_decay(head0 + h)
        dmask = jnp.where((rel >= 0) & (col < n_new), jnp.exp(lg * jnp.maximum(rel, 0.0)), 0.0)
        cross = jnp.exp(lg * (idxf + 1.0))
        kdec = jnp.where(idx < n_new, jnp.exp(lg * (n_new - 1.0 - idxf)), 0.0)
        q = q_ref[0, :, h * DK_R:(h + 1) * DK_R]
        k = k_ref[0, :, h * DK_R:(h + 1) * DK_R]
        v = v_ref[0, :, h * DV_R:(h + 1) * DV_R]
        s0 = s0_ref[0, h]
        sc = lax.dot_general(q, k, (((1,), (1,)), ((), ())), preferred_element_type=F32) * dmask
        o = jnp.dot(sc.astype(BF16), v, preferred_element_type=F32)
        o += jnp.dot(q.astype(F32), s0, preferred_element_type=F32) * jnp.tile(cross, (1, DV_R // LANES))
        kd = (k.astype(F32) * jnp.tile(kdec, (1, DK_R // LANES))).astype(BF16)
        upd = lax.dot_general(kd, v, (((0,), (0,)), ((), ())), preferred_element_type=F32)
        s_ref[0, h] = jnp.exp(lg * float(n_new)) * s0 + upd
        o_ref[0, :, h * DV_R:(h + 1) * DV_R] = (
            _rms(o) * g_ref[0, :, h * DV_R:(h + 1) * DV_R]).astype(BF16)


def _ret_kernel(q_ref, k_ref, v_ref, g_ref, qs_ref, ks_ref, vs_ref, gs_ref, s0_ref,
                o_ref, s_ref, os_ref, ss_ref, state, dmask, cross, kdec, *, n_new, groups):
    h, c = pl.program_id(1), pl.program_id(2)
    _ret_prompt_chunk(h, c, pl.num_programs(2) - 1, q_ref, k_ref, v_ref, g_ref, o_ref, s_ref,
                      state, dmask, cross, kdec)
    step = (pl.program_id(0) * pl.num_programs(1) + h) * pl.num_programs(2) + c
    _ret_sample_heads((step % groups) * RET_SAMPLE_HEADS, n_new, qs_ref, ks_ref, vs_ref, gs_ref,
                      s0_ref, os_ref, ss_ref)


def _ret(qr, kr, vr, gr, qr_s, kr_s, vr_s, gr_s, state0, batch, seq, n_new):
    nc = seq // RET_CHUNK
    nseq = state0.shape[0]
    hs = RET_SAMPLE_HEADS
    groups = H_R // hs
    assert nseq * groups == batch * H_R * nc, "decode sequences must tile the prompt grid"
    flat = lambda b, h, c: (b * H_R + h) * nc + c
    qk = pl.BlockSpec((RET_CHUNK, DK_R), lambda b, h, c: (b * nc + c, h))
    vv = pl.BlockSpec((RET_CHUNK, DV_R), lambda b, h, c: (b * nc + c, h))
    sblk = lambda w: pl.BlockSpec(
        (1, SP, hs * w), lambda b, h, c: (flat(b, h, c) // groups, 0, flat(b, h, c) % groups))
    st = pl.BlockSpec((1, hs, DK_R, DV_R),
                      lambda b, h, c: (flat(b, h, c) // groups, flat(b, h, c) % groups, 0, 0))
    return pl.pallas_call(
        functools.partial(_ret_kernel, n_new=n_new, groups=groups),
        grid=(batch, H_R, nc),
        in_specs=[qk, qk, vv, vv, sblk(DK_R), sblk(DK_R), sblk(DV_R), sblk(DV_R), st],
        out_specs=[vv, pl.BlockSpec((1, 1, DK_R, DV_R), lambda b, h, c: (b, h, 0, 0)),
                   sblk(DV_R), st],
        out_shape=[jax.ShapeDtypeStruct((batch * seq, H_R * DV_R), BF16),
                   jax.ShapeDtypeStruct((batch, H_R, DK_R, DV_R), F32),
                   jax.ShapeDtypeStruct((nseq, SP, H_R * DV_R), BF16),
                   jax.ShapeDtypeStruct(state0.shape, F32)],
        scratch_shapes=[pltpu.VMEM((DK_R, DV_R), F32),
                        pltpu.VMEM((RET_CHUNK, RET_CHUNK), F32),
                        pltpu.VMEM((RET_CHUNK, LANES), F32),
                        pltpu.VMEM((RET_CHUNK, LANES), F32)],
        compiler_params=_params(("arbitrary",) * 3),
        name="ret",
    )(qr, kr, vr, gr, qr_s.reshape(nseq, SP, -1), kr_s.reshape(nseq, SP, -1),
      vr_s.reshape(nseq, SP, -1), gr_s.reshape(nseq, SP, -1), state0)


def _mixout_kernel(oa_ref, or_ref, ga_ref, gg_ref, x_ref, wa_ref, wr_ref, wo_ref, gpost_ref, y_ref):
    ya = jnp.dot(oa_ref[...], wa_ref[...], preferred_element_type=F32)
    yr = jnp.dot(or_ref[...], wr_ref[...], preferred_element_type=F32)
    m = (ga_ref[...] * ya + gg_ref[...] * yr).astype(BF16)
    mo = jnp.dot(m, wo_ref[...], preferred_element_type=F32)
    y_ref[...] = x_ref[...] + _rms(mo) * gpost_ref[...]


def _mixout(oa, o_r, ga, gg, x, wa, wr, wo, g_post):
    n = x.shape[0]
    tm = min(TM_FFN, n)
    row = lambda w: pl.BlockSpec((tm, w), lambda i: (i, 0))
    return pl.pallas_call(
        _mixout_kernel,
        grid=(n // tm,),
        in_specs=[row(D_MODEL), row(2 * D_MODEL), row(D_MODEL), row(D_MODEL), row(D_MODEL),
                  _resident(wa.shape), _resident(wr.shape), _resident(wo.shape),
                  pl.BlockSpec((1, D_MODEL), lambda i: (0, 0))],
        out_specs=row(D_MODEL),
        out_shape=jax.ShapeDtypeStruct((n, D_MODEL), F32),
        compiler_params=_params(("parallel",)),
        name="mix_out",
    )(oa, o_r, ga, gg, x, wa, wr, wo, g_post)


def kernel(x_prompt, x_sample, cache_k, cache_v, state_ret, page_table, norm_pre_ffn1, norm_post_ffn1, w_ffn1_in, w_ffn1_out, norm_pre_mix, norm_post_mix, w_mix_in, lambda_q1, lambda_k1, lambda_q2, lambda_k2, subln_a, w_branch_a, w_branch_r, w_mix_out, norm_pre_ffn2, norm_post_ffn2, w_ffn2_in, w_ffn2_out):
    assert w_mix_in.shape[0] == 1, "single-layer step only"
    batch, seq, _ = x_prompt.shape
    nseq, n_new, _ = x_sample.shape
    past = page_table.shape[1] * cache_k.shape[2]

    def ffn_weights(w_in, w_out):
        return w_in[0].astype(BF16), w_out[0].astype(BF16)

    w1 = ffn_weights(w_ffn1_in, w_ffn1_out)
    w2 = ffn_weights(w_ffn2_in, w_ffn2_out)
    w_mix = w_mix_in[0].astype(BF16)
    wa, wr, wo = (w[0].astype(BF16) for w in (w_branch_a, w_branch_r, w_mix_out))
    lams = (lambda_q1, lambda_k1, lambda_q2, lambda_k2)
    pool, page = cache_k.shape[1], cache_k.shape[2]
    ck = jnp.transpose(cache_k[0], (0, 2, 3, 4, 1)).reshape(pool, D_MODEL, page)
    cv = cache_v[0].reshape(pool, page * H_A, DV_A)

    def pre(x, tabs, period):
        x1, h = _ffn(x, norm_pre_ffn1, norm_post_ffn1, norm_pre_mix, *w1, emit_h=True)
        return x1, _mixin(h, w_mix, tabs, period)

    def post(x1, oa, o_r, ga, gg):
        x2 = _mixout(oa, o_r, ga, gg, x1, wa, wr, wo, norm_post_mix)
        return _ffn(x2, norm_pre_ffn2, norm_post_ffn2, norm_pre_mix, *w2, emit_h=False)[0]

    xs = jnp.pad(x_sample, ((0, 0), (0, SP - n_new), (0, 0))).reshape(nseq * SP, D_MODEL)
    pos_s = past + jnp.arange(nseq * SP) % SP
    x1s, (qa_s, kts, kab_s, vn, _, qr_s, kr_s, vr_s, gr_s, ga_s, gg_s) = pre(
        xs, _rope_tables(pos_s), nseq * SP)
    x1p, (qa_p, ktp, kab_p, vp, vab_p, qr_p, kr_p, vr_p, gr_p, ga_p, gg_p) = pre(
        x_prompt.reshape(batch * seq, D_MODEL), _rope_tables(jnp.arange(seq)), seq)

    oa_p, oa_s = _attn(qa_p, kab_p, vab_p, qa_s, kab_s, vn, ck, cv, page_table, lams, subln_a,
                       batch, seq, n_new)
    or_p, sp, or_s, sn = _ret(qr_p, kr_p, vr_p, gr_p, qr_s, kr_s, vr_s, gr_s, state_ret[0],
                              batch, seq, n_new)

    yp = post(x1p, oa_p, or_p, ga_p, gg_p)
    ys = post(x1s, oa_s.reshape(nseq * SP, D_MODEL), or_s.reshape(nseq * SP, -1), ga_s, gg_s)

    def unpad(a, *tail):
        return a.reshape(nseq, SP, *tail)[:, :n_new]

    k_prompt = jnp.transpose(ktp.reshape(batch, H_A, 2, DK_A, seq), (0, 4, 1, 2, 3))
    k_sample = jnp.transpose(kts.reshape(H_A, 2, DK_A, nseq, SP)[..., :n_new], (3, 4, 0, 1, 2))
    return (yp.reshape(batch, seq, D_MODEL),
            unpad(ys, D_MODEL),
            k_prompt[None],
            vp.reshape(1, batch, seq, H_A, DV_A),
            sp[None],
            k_sample[None],
            unpad(vn, H_A, DV_A)[None],
            sn[None])
```

```python
import functools
import math

import jax
import jax.numpy as jnp
from jax import lax
from jax.experimental import pallas as pl
from jax.experimental.pallas import tpu as pltpu

F32 = jnp.float32
BF16 = jnp.bfloat16

D_MODEL = 1024
D_FF = 2816
DK_A = 64
DV_A = 128
H_A = 8
ROPE_DIM = 16
ROPE_THETA = 500000.0
H_R = 4
DK_R = 256
DV_R = 512
RET_THETA = 10000.0
EPS = 1e-6
LAM_INIT = 0.8 - 0.6 * math.exp(-0.3 * 0)
LOG2E = 1.4426950408889634

LANES = 128
SUBLANES = 8
FF_CHUNK = 256
N_FF_CHUNKS = D_FF // FF_CHUNK
TM_FFN = 512
TM_MIX = 256
TQ = 512
PROMPT_PARTS = 2
RET_CHUNK = 512
RET_SAMPLE_HEADS = 4
SP = SUBLANES
PAGES_PER_STEP = 16
SAMPLE_GROUP = 4
NEG = -1e30
VMEM_LIMIT = 56 * 1024 * 1024


def _params(sem, limit=VMEM_LIMIT):
    return pltpu.CompilerParams(dimension_semantics=sem, vmem_limit_bytes=limit)


def _resident(shape):
    nd = len(shape)
    return pl.BlockSpec(shape, lambda *_: (0,) * nd, pipeline_mode=pl.Buffered(1))


def _rms(x):
    return x * lax.rsqrt(jnp.mean(x * x, axis=-1, keepdims=True) + EPS)


def _ffn_kernel(x_ref, gpre_ref, gpost_ref, gnext_ref, win_ref, wout_ref, *rest, emit_h):
    if emit_h:
        y_ref, h_ref, acc_ref = rest
    else:
        y_ref, acc_ref = rest
    x = x_ref[...]
    xn = (_rms(x) * gpre_ref[...]).astype(BF16)
    acc_ref[...] = jnp.zeros_like(acc_ref)

    for c in range(N_FF_CHUNKS):
        lo = c * FF_CHUNK
        a = jnp.dot(xn, win_ref[:, lo:lo + FF_CHUNK], preferred_element_type=F32)
        b = jnp.dot(xn, win_ref[:, D_FF + lo:D_FF + lo + FF_CHUNK], preferred_element_type=F32)
        hid = (a * jax.nn.sigmoid(a) * b).astype(BF16)
        acc_ref[...] += jnp.dot(hid, wout_ref[lo:lo + FF_CHUNK, :], preferred_element_type=F32)
    y = x + 0.5 * (_rms(acc_ref[...]) * gpost_ref[...])
    y_ref[...] = y
    if emit_h:
        h_ref[...] = (_rms(y) * gnext_ref[...]).astype(BF16)


def _ffn(x, g_pre, g_post, g_next, w_in, w_out, emit_h):
    n = x.shape[0]
    tm = min(TM_FFN, n)
    row = pl.BlockSpec((tm, D_MODEL), lambda i: (i, 0))
    gain = pl.BlockSpec((1, D_MODEL), lambda i: (0, 0))
    out_shape = [jax.ShapeDtypeStruct((n, D_MODEL), F32)]
    out_specs = [row]
    if emit_h:
        out_shape.append(jax.ShapeDtypeStruct((n, D_MODEL), BF16))
        out_specs.append(row)
    return pl.pallas_call(
        functools.partial(_ffn_kernel, emit_h=emit_h),
        grid=(n // tm,),
        in_specs=[row, gain, gain, gain, _resident(w_in.shape), _resident(w_out.shape)],
        out_specs=out_specs,
        out_shape=out_shape,
        scratch_shapes=[pltpu.VMEM((tm, D_MODEL), F32)],
        compiler_params=_params(("parallel",)),
        name="ffn_h" if emit_h else "ffn",
    )(x, g_pre, g_post, g_next, w_in, w_out)


def _mixin_kernel(h_ref, w_ref, ca_ref, sa1_ref, sa2_ref, cr_ref, sr_ref,
                  qa_ref, kt_ref, kab_ref, vaf_ref, vab_ref, qr_ref, kr_ref, vr_ref,
                  gr_ref, ga_ref, gg_ref, kf_scr):
    h = h_ref[...]

    def proj(off, width):
        return jnp.dot(h, w_ref[:, off:off + width], preferred_element_type=F32)

    ca, sa1, sa2 = ca_ref[...], sa1_ref[...], sa2_ref[...]
    cr, sr = cr_ref[...], sr_ref[...]

    def rope_a(z, store):
        for b in range(H_A):
            zb = z[:, b * LANES:(b + 1) * LANES]
            rot = (zb * ca + pltpu.roll(zb, LANES - ROPE_DIM // 2, 1) * sa1
                   + pltpu.roll(zb, ROPE_DIM // 2, 1) * sa2)
            store(b, rot)

    def rope_r(z, scale, out_ref):
        for hh in range(H_R):
            x1 = z[:, hh * DK_R:hh * DK_R + LANES]
            x2 = z[:, hh * DK_R + LANES:(hh + 1) * DK_R]
            out_ref[:, hh * DK_R:hh * DK_R + LANES] = ((x1 * cr - x2 * sr) * scale).astype(BF16)
            out_ref[:, hh * DK_R + LANES:(hh + 1) * DK_R] = ((x2 * cr + x1 * sr) * scale).astype(BF16)

    def store_q(b, rot):
        qa_ref[:, b * LANES:(b + 1) * LANES] = (rot * (DK_A ** -0.5 * LOG2E)).astype(BF16)

    def store_k(b, rot):
        kf_scr[:, b * LANES:(b + 1) * LANES] = rot
        kab_ref[:, b * LANES:(b + 1) * LANES] = rot.astype(BF16)

    rope_a(proj(0, D_MODEL), store_q)
    rope_a(proj(D_MODEL, D_MODEL), store_k)
    kt_ref[0] = kf_scr[...].T
    va = proj(2 * D_MODEL, D_MODEL)
    vaf_ref[...] = va
    vab_ref[...] = va.astype(BF16)
    rope_r(proj(3 * D_MODEL, D_MODEL), 1.0, qr_ref)
    rope_r(proj(4 * D_MODEL, D_MODEL), DK_R ** -0.5, kr_ref)
    vr_ref[...] = proj(5 * D_MODEL, 2 * D_MODEL).astype(BF16)
    g = proj(7 * D_MODEL, 2 * D_MODEL)
    gr_ref[...] = g * jax.nn.sigmoid(g)
    ga_ref[...] = jax.nn.sigmoid(proj(9 * D_MODEL, D_MODEL))
    gg_ref[...] = jax.nn.sigmoid(proj(10 * D_MODEL, D_MODEL))


def _mixin(h, w_mix, tabs, period):
    n = h.shape[0]
    tm = min(TM_MIX, period)
    nper = period // tm
    row = lambda w: pl.BlockSpec((tm, w), lambda i: (i, 0))
    tab = pl.BlockSpec((tm, LANES), lambda i: (i % nper, 0))
    widths = [(D_MODEL, BF16), None, (D_MODEL, BF16), (D_MODEL, F32), (D_MODEL, BF16),
              (D_MODEL, BF16), (D_MODEL, BF16), (2 * D_MODEL, BF16), (2 * D_MODEL, F32),
              (D_MODEL, F32), (D_MODEL, F32)]
    out_specs = [row(w[0]) if w else
                 pl.BlockSpec((1, D_MODEL, tm), lambda i: (i // nper, 0, i % nper)) for w in widths]
    out_shape = [jax.ShapeDtypeStruct((n, w[0]), w[1]) if w else
                 jax.ShapeDtypeStruct((n // period, D_MODEL, period), F32) for w in widths]
    return pl.pallas_call(
        _mixin_kernel,
        grid=(n // tm,),
        in_specs=[row(D_MODEL), _resident(w_mix.shape)] + [tab] * 5,
        out_specs=out_specs,
        out_shape=out_shape,
        scratch_shapes=[pltpu.VMEM((tm, D_MODEL), F32)],
        compiler_params=_params(("parallel",), 60 * 1024 * 1024),
        name="mix_in",
    )(h, w_mix, *tabs)


def _rope_tables(pos):
    posf = pos.astype(F32)[:, None]
    n = pos.shape[0]
    half = ROPE_DIM // 2
    inv = jnp.power(ROPE_THETA, -jnp.arange(half, dtype=F32) * (2.0 / ROPE_DIM))
    ang = posf * inv[None, :]
    c, s = jnp.cos(ang), jnp.sin(ang)
    ones = jnp.ones((n, DK_A - ROPE_DIM), F32)
    z8 = jnp.zeros((n, half), F32)
    zrest = jnp.zeros((n, DK_A - ROPE_DIM), F32)
    ca = jnp.tile(jnp.concatenate([c, c, ones], axis=1), (1, 2))
    sa1 = jnp.tile(jnp.concatenate([-s, z8, zrest], axis=1), (1, 2))
    sa2 = jnp.tile(jnp.concatenate([z8, s, zrest], axis=1), (1, 2))
    inv_r = jnp.power(RET_THETA, -jnp.arange(DK_R // 2, dtype=F32) * (2.0 / DK_R))
    ang_r = posf * inv_r[None, :]
    return ca, sa1, sa2, jnp.cos(ang_r), jnp.sin(ang_r)


def _lambda(lq1_ref, lk1_ref, lq2_ref, lk2_ref):
    a = jnp.sum(lq1_ref[...] * lk1_ref[...], axis=-1, keepdims=True)
    b = jnp.sum(lq2_ref[...] * lk2_ref[...], axis=-1, keepdims=True)
    return jnp.exp(a) - jnp.exp(b) + LAM_INIT


def _prompt_attn(qi, part, q_ref, k_ref, v_ref, lam, gain, o_ref, q2_scr, m_scr, l_scr, acc_scr):
    @pl.when(part == 0)
    def _():
        q = q_ref[...]
        lane = lax.broadcasted_iota(jnp.int32, q.shape, 1)
        zero = jnp.zeros_like(q)
        q2_scr[:TQ, :] = jnp.where(lane < DK_A, q, zero)
        q2_scr[TQ:, :] = jnp.where(lane >= DK_A, q, zero)
        m_scr[...] = jnp.full(m_scr.shape, NEG, F32)
        l_scr[...] = jnp.zeros_like(l_scr)
        acc_scr[...] = jnp.zeros_like(acc_scr)

    def step(tile, masked):
        off = pl.multiple_of(tile * TQ, TQ)
        k = k_ref[pl.ds(off, TQ), :]
        v = v_ref[pl.ds(off, TQ), :]
        s = lax.dot_general(q2_scr[...], k, (((1,), (1,)), ((), ())),
                            preferred_element_type=F32)
        if masked:
            row = lax.broadcasted_iota(jnp.int32, s.shape, 0)
            col = lax.broadcasted_iota(jnp.int32, s.shape, 1)
            s = jnp.where(col <= (row & (TQ - 1)), s, NEG)
        m_prev = m_scr[...]
        m_next = jnp.maximum(m_prev, jnp.max(s, axis=1, keepdims=True))
        p = jnp.exp2(s - jnp.tile(m_next, (1, TQ // LANES)))
        alpha = jnp.exp2(m_prev - m_next)
        l_scr[...] = alpha * l_scr[...] + jnp.sum(p, axis=1, keepdims=True)
        acc_scr[...] = alpha * acc_scr[...] + jnp.dot(p.astype(BF16), v,
                                                      preferred_element_type=F32)
        m_scr[...] = m_next

    half = (qi + 1) // 2
    lo = part * half
    hi = half + part * (qi - half)

    def pair(jj, carry):
        step(lo + 2 * jj, False)
        step(lo + 2 * jj + 1, False)
        return carry

    lax.fori_loop(0, (hi - lo) // 2, pair, 0)

    @pl.when((hi - lo) % 2 == 1)
    def _():
        step(hi - 1, False)

    @pl.when(part == 1)
    def _():
        step(qi, True)
        o = acc_scr[:TQ, :] / l_scr[:TQ, :] - lam * (acc_scr[TQ:, :] / l_scr[TQ:, :])
        o_ref[...] = (_rms(o) * gain).astype(BF16)


def _sample_attn(sub, nsub, n_new, q_ref, kn_ref, vn_ref, kp, vp, lam, gain, o_ref,
                 wq_scr, m_scr, l_scr, acc_scr):
    page = kp[0].shape[2]
    hrows = lambda h: slice(h * SP, (h + 1) * SP)
    hlanes = lambda h: slice(h * LANES, (h + 1) * LANES)
    row8 = lax.broadcasted_iota(jnp.int32, (SP, LANES), 0)

    @pl.when(sub == 0)
    def _():
        qf = q_ref[0].astype(F32)
        row = lax.broadcasted_iota(jnp.int32, qf.shape, 0)
        qq = jnp.where(row < n_new, qf, pltpu.roll(qf, n_new, 0))
        lane = lax.broadcasted_iota(jnp.int32, (SP, LANES), 1)
        own = (lane // DK_A) == (row8 // n_new)
        for h in range(H_A):
            wq_scr[hrows(h), :] = jnp.where(own, qq[:, hlanes(h)], 0.0)
        m_scr[...] = jnp.full(m_scr.shape, NEG, F32)
        l_scr[...] = jnp.zeros_like(l_scr)
        acc_scr[...] = jnp.zeros_like(acc_scr)

    wq = wq_scr[...]

    def scores(pages):
        return jnp.concatenate(
            [jnp.concatenate(
                [jnp.dot(wq[hrows(h), :],
                         jnp.concatenate([kp[i][0, hlanes(h), :], kp[i + 1][0, hlanes(h), :]], axis=1),
                         preferred_element_type=F32)
                 for i in pages[::2]], axis=1)
             for h in range(H_A)], axis=0)

    groups = [range(g, g + SAMPLE_GROUP) for g in range(0, len(kp), SAMPLE_GROUP)]
    m, l, acc = m_scr[...], l_scr[...], acc_scr[...]
    s_next = scores(groups[0])
    for g, pages in enumerate(groups):
        s = s_next
        if g + 1 < len(groups):
            s_next = scores(groups[g + 1])
        m_next = jnp.maximum(m, jnp.max(s, axis=1, keepdims=True))
        p = jnp.exp2(s - jnp.tile(m_next, (1, len(pages))))
        alpha = jnp.exp2(m - m_next)
        l = alpha * l + jnp.sum(p, axis=1, keepdims=True)
        m = m_next
        pv = []
        for h in range(0, H_A, 2):
            d = None
            for n, i in enumerate(pages):
                v2 = jnp.concatenate([vp[i][0, pl.ds(h, page, stride=H_A), :],
                                      vp[i][0, pl.ds(h + 1, page, stride=H_A), :]], axis=1)
                t = jnp.dot(p[h * SP:(h + 2) * SP, n * page:(n + 1) * page], v2,
                            preferred_element_type=F32)
                d = t if d is None else d + t
            pv += [d[:SP, :LANES], d[SP:, LANES:]]
        acc = alpha * acc + jnp.concatenate(pv, axis=0)
    m_scr[...], l_scr[...], acc_scr[...] = m, l, acc

    @pl.when(sub == nsub - 1)
    def _():
        kn = kn_ref[0].astype(F32)
        vn = vn_ref[0]
        t_row = row8 % n_new
        for h in range(H_A):
            wq_h = wq_scr[hrows(h), :]
            s_new = []
            for jj in range(n_new):
                sj = jnp.sum(wq_h * kn[jj:jj + 1, hlanes(h)], axis=1, keepdims=True)
                s_new.append(jnp.where(t_row >= jj, jnp.broadcast_to(sj, (SP, LANES)), NEG))
            m_prev = m_scr[hrows(h), :]
            m_next = m_prev
            for sj in s_new:
                m_next = jnp.maximum(m_next, sj)
            alpha = jnp.exp2(m_prev - m_next)
            l = alpha * l_scr[hrows(h), :]
            acc = alpha * acc_scr[hrows(h), :]
            for jj in range(n_new):
                pj = jnp.exp2(s_new[jj] - m_next)
                l = l + pj
                acc = acc + pj * vn[jj:jj + 1, hlanes(h)]
            b = acc * (jnp.where(row8 < n_new, 1.0, -lam) / l)
            o = b + pltpu.roll(b, n_new, 0)
            o_ref[0, :, hlanes(h)] = (_rms(o) * gain).astype(BF16)


def _attn_kernel(pt_ref, q_ref, k_ref, v_ref, qs_ref, kn_ref, vn_ref, *rest, n_new, nsub, nq):
    pps = PAGES_PER_STEP
    kp, vp = rest[:pps], rest[pps:2 * pps]
    (lq1_ref, lk1_ref, lq2_ref, lk2_ref, g_ref, o_ref, os_ref,
     q2_scr, m_scr, l_scr, acc_scr, wq_scr, ms_scr, ls_scr, accs_scr) = rest[2 * pps:]
    step = pl.program_id(0)
    qi, part = (step // PROMPT_PARTS) % nq, step % PROMPT_PARTS
    lam = _lambda(lq1_ref, lk1_ref, lq2_ref, lk2_ref)
    gain = g_ref[...] * (1.0 - LAM_INIT)
    _prompt_attn(qi, part, q_ref, k_ref, v_ref, lam, gain, o_ref, q2_scr, m_scr, l_scr, acc_scr)
    _sample_attn(step % nsub, nsub, n_new, qs_ref, kn_ref, vn_ref, kp, vp, lam, gain, os_ref,
                 wq_scr, ms_scr, ls_scr, accs_scr)


def _attn(qa, kab, vab, qa_s, kab_s, vaf_s, cache_kt, cache_v2, page_table, lams, subln,
          batch, seq, n_new):
    assert 2 * n_new == SP and cache_kt.shape[2] == LANES
    nseq, npages = page_table.shape
    pps = PAGES_PER_STEP
    nq = seq // TQ
    nsteps = batch * H_A * nq * PROMPT_PARTS
    nsub = npages // pps
    assert npages % pps == 0 and nseq * nsub == nsteps, "decode pages must tile the prompt grid"
    tile = lambda t: ((t // (PROMPT_PARTS * nq * H_A)) * nq + (t // PROMPT_PARTS) % nq,
                      (t // (PROMPT_PARTS * nq)) % H_A)
    bh = lambda t: (t // (PROMPT_PARTS * nq * H_A), (t // (PROMPT_PARTS * nq)) % H_A)
    head = lambda rows, imap: pl.BlockSpec((rows, LANES), imap)
    seqblk = pl.BlockSpec((1, SP, D_MODEL), lambda t, pt: (t // nsub, 0, 0))
    small = lambda w: pl.BlockSpec((1, w), lambda t, pt: (0, 0))

    def page(arr, n):
        return pl.BlockSpec((1,) + arr.shape[1:], lambda t, pt: (pt[t * pps + n], 0, 0))

    grid_spec = pltpu.PrefetchScalarGridSpec(
        num_scalar_prefetch=1,
        grid=(nsteps,),
        in_specs=[head(TQ, lambda t, pt: tile(t)),
                  head(seq, lambda t, pt: bh(t)),
                  head(seq, lambda t, pt: bh(t)),
                  seqblk, seqblk, seqblk]
                 + [page(cache_kt, n) for n in range(pps)]
                 + [page(cache_v2, n) for n in range(pps)]
                 + [small(DK_A)] * 4 + [small(DV_A)],
        out_specs=[head(TQ, lambda t, pt: tile(t)), seqblk],
        scratch_shapes=[pltpu.VMEM((2 * TQ, LANES), BF16),
                        pltpu.VMEM((2 * TQ, LANES), F32),
                        pltpu.VMEM((2 * TQ, LANES), F32),
                        pltpu.VMEM((2 * TQ, DV_A), F32)]
                       + [pltpu.VMEM((H_A * SP, LANES), F32)] * 4)
    return pl.pallas_call(
        functools.partial(_attn_kernel, n_new=n_new, nsub=nsub, nq=nq),
        grid_spec=grid_spec,
        out_shape=[jax.ShapeDtypeStruct((batch * seq, H_A * DV_A), BF16),
                   jax.ShapeDtypeStruct((nseq, SP, D_MODEL), BF16)],
        compiler_params=_params(("arbitrary",)),
        name="attn",
    )(page_table.reshape(-1), qa, kab, vab, qa_s.reshape(nseq, SP, D_MODEL), kab_s.reshape(nseq, SP, D_MODEL),
      vaf_s.reshape(nseq, SP, D_MODEL), *([cache_kt] * pps), *([cache_v2] * pps), *lams, subln)


def _head_log_decay(h):
    hv = jnp.zeros((1, 1), F32) + h.astype(F32)
    return jnp.log(1.0 - jnp.exp2(-5.0 - hv))


def _ret_prompt_chunk(h, c, last, q_ref, k_ref, v_ref, g_ref, o_ref, s_ref, state, dmask, cross, kdec):
    L = RET_CHUNK

    @pl.when(c == 0)
    def _():
        lg = _head_log_decay(h)
        row = lax.broadcasted_iota(jnp.int32, (L, L), 0)
        col = lax.broadcasted_iota(jnp.int32, (L, L), 1)
        rel = (row - col).astype(F32)
        dmask[...] = jnp.where(rel >= 0, jnp.exp(lg * jnp.maximum(rel, 0.0)), 0.0)
        idx = lax.broadcasted_iota(jnp.int32, (L, LANES), 0).astype(F32)
        cross[...] = jnp.exp(lg * (idx + 1.0))
        kdec[...] = jnp.exp(lg * (L - 1.0 - idx))
        state[...] = jnp.zeros_like(state)

    q, k, v = q_ref[...], k_ref[...], v_ref[...]
    sc = lax.dot_general(q, k, (((1,), (1,)), ((), ())), preferred_element_type=F32) * dmask[...]
    s_old = state[...]
    o = jnp.dot(sc.astype(BF16), v, preferred_element_type=F32)
    o += jnp.dot(q, s_old.astype(BF16), preferred_element_type=F32) * jnp.tile(cross[...], (1, DV_R // LANES))
    kd = (k.astype(F32) * jnp.tile(kdec[...], (1, DK_R // LANES))).astype(BF16)
    upd = lax.dot_general(kd, v, (((0,), (0,)), ((), ())), preferred_element_type=F32)
    state[...] = cross[L - 1:L, :1] * s_old + upd
    o_ref[...] = (_rms(o) * g_ref[...]).astype(BF16)

    @pl.when(c == last)
    def _():
        s_ref[0, 0] = state[...]


def _ret_sample_heads(head0, n_new, q_ref, k_ref, v_ref, g_ref, s0_ref, o_ref, s_ref):
    row = lax.broadcasted_iota(jnp.int32, (SP, SP), 0)
    col = lax.broadcasted_iota(jnp.int32, (SP, SP), 1)
    rel = (row - col).astype(F32)
    idx = lax.broadcasted_iota(jnp.int32, (SP, LANES), 0)
    idxf = idx.astype(F32)
    for h in range(s0_ref.shape[1]):
        lg = _head_log_decay(head0 + h)
        dmask = jnp.where((rel >= 0) & (col < n_new), jnp.exp(lg * jnp.maximum(rel, 0.0)), 0.0)
        cross = jnp.exp(lg * (idxf + 1.0))
        kdec = jnp.where(idx < n_new, jnp.exp(lg * (n_new - 1.0 - idxf)), 0.0)
        q = q_ref[0, :, h * DK_R:(h + 1) * DK_R]
        k = k_ref[0, :, h * DK_R:(h + 1) * DK_R]
        v = v_ref[0, :, h * DV_R:(h + 1) * DV_R]
        s0 = s0_ref[0, h]
        sc = lax.dot_general(q, k, (((1,), (1,)), ((), ())), preferred_element_type=F32) * dmask
        o = jnp.dot(sc.astype(BF16), v, preferred_element_type=F32)
        o += jnp.dot(q.astype(F32), s0, preferred_element_type=F32) * jnp.tile(cross, (1, DV_R // LANES))
        kd = (k.astype(F32) * jnp.tile(kdec, (1, DK_R // LANES))).astype(BF16)
        upd = lax.dot_general(kd, v, (((0,), (0,)), ((), ())), preferred_element_type=F32)
        s_ref[0, h] = jnp.exp(lg * float(n_new)) * s0 + upd
        o_ref[0, :, h * DV_R:(h + 1) * DV_R] = (
            _rms(o) * g_ref[0, :, h * DV_R:(h + 1) * DV_R]).astype(BF16)


def _ret_kernel(q_ref, k_ref, v_ref, g_ref, qs_ref, ks_ref, vs_ref, gs_ref, s0_ref,
                o_ref, s_ref, os_ref, ss_ref, state, dmask, cross, kdec, *, n_new, groups):
    h, c = pl.program_id(1), pl.program_id(2)
    _ret_prompt_chunk(h, c, pl.num_programs(2) - 1, q_ref, k_ref, v_ref, g_ref, o_ref, s_ref,
                      state, dmask, cross, kdec)
    step = (pl.program_id(0) * pl.num_programs(1) + h) * pl.num_programs(2) + c
    _ret_sample_heads((step % groups) * RET_SAMPLE_HEADS, n_new, qs_ref, ks_ref, vs_ref, gs_ref,
                      s0_ref, os_ref, ss_ref)


def _ret(qr, kr, vr, gr, qr_s, kr_s, vr_s, gr_s, state0, batch, seq, n_new):
    nc = seq // RET_CHUNK
    nseq = state0.shape[0]
    hs = RET_SAMPLE_HEADS
    groups = H_R // hs
    assert nseq * groups == batch * H_R * nc, "decode sequences must tile the prompt grid"
    flat = lambda b, h, c: (b * H_R + h) * nc + c
    qk = pl.BlockSpec((RET_CHUNK, DK_R), lambda b, h, c: (b * nc + c, h))
    vv = pl.BlockSpec((RET_CHUNK, DV_R), lambda b, h, c: (b * nc + c, h))
    sblk = lambda w: pl.BlockSpec(
        (1, SP, hs * w), lambda b, h, c: (flat(b, h, c) // groups, 0, flat(b, h, c) % groups))
    st = pl.BlockSpec((1, hs, DK_R, DV_R),
                      lambda b, h, c: (flat(b, h, c) // groups, flat(b, h, c) % groups, 0, 0))
    return pl.pallas_call(
        functools.partial(_ret_kernel, n_new=n_new, groups=groups),
        grid=(batch, H_R, nc),
        in_specs=[qk, qk, vv, vv, sblk(DK_R), sblk(DK_R), sblk(DV_R), sblk(DV_R), st],
        out_specs=[vv, pl.BlockSpec((1, 1, DK_R, DV_R), lambda b, h, c: (b, h, 0, 0)),
                   sblk(DV_R), st],
        out_shape=[jax.ShapeDtypeStruct((batch * seq, H_R * DV_R), BF16),
                   jax.ShapeDtypeStruct((batch, H_R, DK_R, DV_R), F32),
                   jax.ShapeDtypeStruct((nseq, SP, H_R * DV_R), BF16),
                   jax.ShapeDtypeStruct(state0.shape, F32)],
        scratch_shapes=[pltpu.VMEM((DK_R, DV_R), F32),
                        pltpu.VMEM((RET_CHUNK, RET_CHUNK), F32),
                        pltpu.VMEM((RET_CHUNK, LANES), F32),
                        pltpu.VMEM((RET_CHUNK, LANES), F32)],
        compiler_params=_params(("arbitrary",) * 3),
        name="ret",
    )(qr, kr, vr, gr, qr_s.reshape(nseq, SP, -1), kr_s.reshape(nseq, SP, -1),
      vr_s.reshape(nseq, SP, -1), gr_s.reshape(nseq, SP, -1), state0)


def _mixout_kernel(oa_ref, or_ref, ga_ref, gg_ref, x_ref, wa_ref, wr_ref, wo_ref, gpost_ref, y_ref):
    ya = jnp.dot(oa_ref[...], wa_ref[...], preferred_element_type=F32)
    yr = jnp.dot(or_ref[...], wr_ref[...], preferred_element_type=F32)
    m = (ga_ref[...] * ya + gg_ref[...] * yr).astype(BF16)
    mo = jnp.dot(m, wo_ref[...], preferred_element_type=F32)
    y_ref[...] = x_ref[...] + _rms(mo) * gpost_ref[...]


def _mixout(oa, o_r, ga, gg, x, wa, wr, wo, g_post):
    n = x.shape[0]
    tm = min(TM_FFN, n)
    row = lambda w: pl.BlockSpec((tm, w), lambda i: (i, 0))
    return pl.pallas_call(
        _mixout_kernel,
        grid=(n // tm,),
        in_specs=[row(D_MODEL), row(2 * D_MODEL), row(D_MODEL), row(D_MODEL), row(D_MODEL),
                  _resident(wa.shape), _resident(wr.shape), _resident(wo.shape),
                  pl.BlockSpec((1, D_MODEL), lambda i: (0, 0))],
        out_specs=row(D_MODEL),
        out_shape=jax.ShapeDtypeStruct((n, D_MODEL), F32),
        compiler_params=_params(("parallel",)),
        name="mix_out",
    )(oa, o_r, ga, gg, x, wa, wr, wo, g_post)


def kernel(x_prompt, x_sample, cache_k, cache_v, state_ret, page_table, norm_pre_ffn1, norm_post_ffn1, w_ffn1_in, w_ffn1_out, norm_pre_mix, norm_post_mix, w_mix_in, lambda_q1, lambda_k1, lambda_q2, lambda_k2, subln_a, w_branch_a, w_branch_r, w_mix_out, norm_pre_ffn2, norm_post_ffn2, w_ffn2_in, w_ffn2_out):
    assert w_mix_in.shape[0] == 1, "single-layer step only"
    batch, seq, _ = x_prompt.shape
    nseq, n_new, _ = x_sample.shape
    past = page_table.shape[1] * cache_k.shape[2]

    def ffn_weights(w_in, w_out):
        return w_in[0].astype(BF16), w_out[0].astype(BF16)

    w1 = ffn_weights(w_ffn1_in, w_ffn1_out)
    w2 = ffn_weights(w_ffn2_in, w_ffn2_out)
    w_mix = w_mix_in[0].astype(BF16)
    wa, wr, wo = (w[0].astype(BF16) for w in (w_branch_a, w_branch_r, w_mix_out))
    lams = (lambda_q1, lambda_k1, lambda_q2, lambda_k2)
    pool, page = cache_k.shape[1], cache_k.shape[2]
    ck = jnp.transpose(cache_k[0], (0, 2, 3, 4, 1)).reshape(pool, D_MODEL, page)
    cv = cache_v[0].reshape(pool, page * H_A, DV_A)

    def pre(x, tabs, period):
        x1, h = _ffn(x, norm_pre_ffn1, norm_post_ffn1, norm_pre_mix, *w1, emit_h=True)
        return x1, _mixin(h, w_mix, tabs, period)

    def post(x1, oa, o_r, ga, gg):
        x2 = _mixout(oa, o_r, ga, gg, x1, wa, wr, wo, norm_post_mix)
        return _ffn(x2, norm_pre_ffn2, norm_post_ffn2, norm_pre_mix, *w2, emit_h=False)[0]

    xs = jnp.pad(x_sample, ((0, 0), (0, SP - n_new), (0, 0))).reshape(nseq * SP, D_MODEL)
    pos_s = past + jnp.arange(nseq * SP) % SP
    x1s, (qa_s, kts, kab_s, vn, _, qr_s, kr_s, vr_s, gr_s, ga_s, gg_s) = pre(
        xs, _rope_tables(pos_s), nseq * SP)
    x1p, (qa_p, ktp, kab_p, vp, vab_p, qr_p, kr_p, vr_p, gr_p, ga_p, gg_p) = pre(
        x_prompt.reshape(batch * seq, D_MODEL), _rope_tables(jnp.arange(seq)), seq)

    oa_p, oa_s = _attn(qa_p, kab_p, vab_p, qa_s, kab_s, vn, ck, cv, page_table, lams, subln_a,
                       batch, seq, n_new)
    or_p, sp, or_s, sn = _ret(qr_p, kr_p, vr_p, gr_p, qr_s, kr_s, vr_s, gr_s, state_ret[0],
                              batch, seq, n_new)

    yp = post(x1p, oa_p, or_p, ga_p, gg_p)
    ys = post(x1s, oa_s.reshape(nseq * SP, D_MODEL), or_s.reshape(nseq * SP, -1), ga_s, gg_s)

    def unpad(a, *tail):
        return a.reshape(nseq, SP, *tail)[:, :n_new]

    k_prompt = jnp.transpose(ktp.reshape(batch, H_A, 2, DK_A, seq), (0, 4, 1, 2, 3))
    k_sample = jnp.transpose(kts.reshape(H_A, 2, DK_A, nseq, SP)[..., :n_new], (3, 4, 0, 1, 2))
    return (yp.reshape(batch, seq, D_MODEL),
            unpad(ys, D_MODEL),
            k_prompt[None],
            vp.reshape(1, batch, seq, H_A, DV_A),
            sp[None],
            k_sample[None],
            unpad(vn, H_A, DV_A)[None],
            sn[None])
```
